```python
import math
import jax
import jax.numpy as jnp
from jax import lax
import numpy as np


D_MODEL = 1024
BATCH = 8
SEQ = 4096
DEPTH = 2

HEAD_DIM = 64
FOX_HEADS = 4
SWA_HEADS = 4
SWA_KV_HEADS = 2
SWA_WINDOW = 128
DIFF_HEADS = 4
DIFF_QK_DIM = 32
DIL_HEADS = 4
DIL_PATTERNS = ((128, 1), (512, 4), (2048, 16))
N_BRANCHES = 4
BRANCH_WIDTH = 4 * HEAD_DIM
QUERY_BLOCK = 128
BAND_BLOCK = 128
BAND_PAD = 128
T5_BUCKETS = 32
T5_MAX_DIST = 2048
T5_HEADS = SWA_HEADS + DIFF_HEADS + DIL_HEADS
N_EXPERTS = 64
TOP_K = 8
N_GROUPS = 8
TOPK_GROUPS = 4
EXPERT_DIM = 256
SHARED_DIM = 256
ROUTED_SCALE = 2.5
MOE_BLOCK = 128
ALPHA = (2 * DEPTH) ** 0.25
BETA = (8 * DEPTH) ** -0.25
LN_EPS = 1e-5
IN_SPLITS = (
    FOX_HEADS * HEAD_DIM, FOX_HEADS * HEAD_DIM, FOX_HEADS * HEAD_DIM, FOX_HEADS,
    SWA_HEADS * HEAD_DIM, SWA_KV_HEADS * HEAD_DIM, SWA_KV_HEADS * HEAD_DIM,
    DIFF_HEADS * 2 * DIFF_QK_DIM, DIFF_HEADS * 2 * DIFF_QK_DIM, DIFF_HEADS * HEAD_DIM,
    DIL_HEADS * HEAD_DIM, DIL_HEADS * HEAD_DIM, DIL_HEADS * HEAD_DIM,
    N_BRANCHES * D_MODEL,
)
D_IN = sum(IN_SPLITS)

kernel_name = 'hybrid_fox_swa_diff_dilated_moe_deepnorm'


def layer_norm(x, g, b):
    xf = x.astype(jnp.float32)
    mu = jnp.mean(xf, axis=-1, keepdims=True)
    var = jnp.mean(jnp.square(xf - mu), axis=-1, keepdims=True)
    y = (xf - mu) * lax.rsqrt(var + LN_EPS)
    return (y * g.astype(jnp.float32) + b.astype(jnp.float32)).astype(x.dtype)


def t5_bucket(dist):
    n = jnp.maximum(dist, 0)
    exact = T5_BUCKETS // 2
    nf = jnp.maximum(n, 1).astype(jnp.float32)
    large = exact + (jnp.log(nf / exact) / math.log(T5_MAX_DIST / exact) * (T5_BUCKETS - exact)).astype(jnp.int32)
    large = jnp.minimum(large, T5_BUCKETS - 1)
    return jnp.where(n < exact, n, large)


def forgetting_attention(q, k, v, log_f):
    bsz, seq, heads, dh = q.shape
    nb = seq // QUERY_BLOCK
    c = jnp.cumsum(log_f, axis=1)
    c_keys = c.transpose(0, 2, 1)
    kpos = jnp.arange(seq)
    scale = dh ** -0.5

    def block(args):
        qi, ci, i = args
        qpos = i * QUERY_BLOCK + jnp.arange(QUERY_BLOCK)
        s = jnp.einsum('bqhd,bkhd->bhqk', qi, k) * scale
        s = s + (ci.transpose(0, 2, 1)[:, :, :, None] - c_keys[:, :, None, :])
        s = jnp.where(kpos[None, :] <= qpos[:, None], s, -jnp.inf)
        return jnp.einsum('bhqk,bkhd->bqhd', jax.nn.softmax(s, axis=-1), v)

    qb = q.reshape(bsz, nb, QUERY_BLOCK, heads, dh).swapaxes(0, 1)
    cb = c.reshape(bsz, nb, QUERY_BLOCK, heads).swapaxes(0, 1)
    o = lax.map(block, (qb, cb, jnp.arange(nb)))
    return o.swapaxes(0, 1).reshape(bsz, seq, heads, dh)


def differential_attention(q1, q2, k1, k2, v, lam, table):
    bsz, seq, heads, dq = q1.shape
    nb = seq // QUERY_BLOCK
    kpos = jnp.arange(seq)
    scale = dq ** -0.5

    def block(args):
        q1i, q2i, i = args
        qpos = i * QUERY_BLOCK + jnp.arange(QUERY_BLOCK)
        dist = qpos[:, None] - kpos[None, :]
        bias = table[t5_bucket(dist)].transpose(2, 0, 1)
        valid = dist >= 0

        def probs(qi, ki):
            s = jnp.einsum('bqhd,bkhd->bhqk', qi, ki) * scale + bias
            return jax.nn.softmax(jnp.where(valid, s, -jnp.inf), axis=-1)

        p = probs(q1i, k1) - lam * probs(q2i, k2)
        return jnp.einsum('bhqk,bkhd->bqhd', p, v)

    def to_blocks(t):
        return t.reshape(bsz, nb, QUERY_BLOCK, heads, t.shape[-1]).swapaxes(0, 1)

    o = lax.map(block, (to_blocks(q1), to_blocks(q2), jnp.arange(nb)))
    return o.swapaxes(0, 1).reshape(bsz, seq, heads, v.shape[-1])


def banded_attention(q, k, v, max_dist, dilation, table, sinks=None):
    bp, seq_len, hq, dh = q.shape
    hkv = k.shape[2]
    rep = hq // hkv
    blk = math.gcd(seq_len, BAND_BLOCK)
    nb = seq_len // blk
    kw = blk + BAND_PAD
    qb = q.reshape(bp, nb, blk, hkv, rep, dh)
    pad = ((0, 0), (BAND_PAD, 0), (0, 0), (0, 0))
    idx = jnp.arange(nb)[:, None] * blk + jnp.arange(kw)[None, :]
    ks = jnp.pad(k, pad)[:, idx]
    vs = jnp.pad(v, pad)[:, idx]
    qpos = jnp.arange(nb)[:, None] * blk + jnp.arange(blk)[None, :]
    kpos = idx - BAND_PAD
    dist = qpos[:, :, None] - kpos[:, None, :]
    valid = (dist >= 0) & (dist <= max_dist) & (kpos[:, None, :] >= 0)
    bias = table[t5_bucket(dist * dilation)]
    bias = bias.reshape(nb, blk, kw, hkv, rep).transpose(0, 3, 4, 1, 2)
    s = jnp.einsum('bnqgrd,bnkgd->bngrqk', qb, ks) * (dh ** -0.5) + bias
    s = jnp.where(valid[:, None, None], s, -jnp.inf)
    m = jnp.max(s, axis=-1)
    if sinks is not None:
        sk = sinks.reshape(hkv, rep)[:, :, None]
        m = jnp.maximum(m, sk)
    p = jnp.exp(s - m[..., None])
    den = jnp.sum(p, axis=-1)
    if sinks is not None:
        den = den + jnp.exp(sk - m)
    o = jnp.einsum('bngrqk,bnkgd->bnqgrd', p, vs) / den.transpose(0, 1, 4, 2, 3)[..., None]
    lse = (m + jnp.log(den)).transpose(0, 1, 4, 2, 3).reshape(bp, seq_len, hq)
    return o.reshape(bp, seq_len, hq, dh), lse


def to_residues(t, dil):
    b, s = t.shape[0], t.shape[1]
    return t.reshape(b, s // dil, dil, *t.shape[2:]).swapaxes(1, 2).reshape(b * dil, s // dil, *t.shape[2:])


def from_residues(t, b, dil):
    s_sub = t.shape[1]
    return t.reshape(b, dil, s_sub, *t.shape[2:]).swapaxes(1, 2).reshape(b, s_sub * dil, *t.shape[2:])


def dilated_attention(q, k, v, table):
    bsz = q.shape[0]
    outs, lses = [], []
    for window, dil in DIL_PATTERNS:
        o, lse = banded_attention(to_residues(q, dil), to_residues(k, dil), to_residues(v, dil),
                                  window // dil, dil, table)
        outs.append(from_residues(o, bsz, dil))
        lses.append(from_residues(lse, bsz, dil))
    wts = jax.nn.softmax(jnp.stack(lses), axis=0)
    return jnp.einsum('pbsh,pbshd->bshd', wts, jnp.stack(outs))


def mixer_sublayer(x, t5_table, w_in, b_forget, sinks, lam_q1, lam_k1, lam_q2, lam_k2,
                   diff_norm_g, w_branch, w_out, lam_init):
    bsz, seq, _ = x.shape
    f32 = jnp.float32
    h = x @ w_in
    offs = [int(o) for o in np.cumsum(IN_SPLITS)[:-1]]
    (qa, ka, va, fa, qb, kb, vb, qc, kc, vc, qd, kd, vd, gates) = jnp.split(h, offs, axis=-1)

    def heads(t, n):
        return t.astype(f32).reshape(bsz, seq, n, -1)

    tab = t5_table.astype(f32)
    log_f = jax.nn.log_sigmoid(fa.astype(f32) + b_forget.astype(f32))
    ya = forgetting_attention(heads(qa, FOX_HEADS), heads(ka, FOX_HEADS), heads(va, FOX_HEADS), log_f)
    yb, _ = banded_attention(heads(qb, SWA_HEADS), heads(kb, SWA_KV_HEADS), heads(vb, SWA_KV_HEADS),
                             SWA_WINDOW - 1, 1, tab[:, :SWA_HEADS], sinks.astype(f32))
    qc = heads(qc, DIFF_HEADS).reshape(bsz, seq, DIFF_HEADS, 2, DIFF_QK_DIM)
    kc = heads(kc, DIFF_HEADS).reshape(bsz, seq, DIFF_HEADS, 2, DIFF_QK_DIM)
    lam = (jnp.exp(jnp.sum(lam_q1.astype(f32) * lam_k1.astype(f32)))
           - jnp.exp(jnp.sum(lam_q2.astype(f32) * lam_k2.astype(f32))) + lam_init)
    yc = differential_attention(qc[:, :, :, 0], qc[:, :, :, 1], kc[:, :, :, 0], kc[:, :, :, 1],
                                heads(vc, DIFF_HEADS), lam, tab[:, SWA_HEADS:SWA_HEADS + DIFF_HEADS])
    yc = yc * lax.rsqrt(jnp.mean(yc * yc, axis=-1, keepdims=True) + LN_EPS) * diff_norm_g.astype(f32) * (1.0 - lam_init)
    yd = dilated_attention(heads(qd, DIL_HEADS), heads(kd, DIL_HEADS), heads(vd, DIL_HEADS),
                           tab[:, SWA_HEADS + DIFF_HEADS:])
    gates = gates.reshape(bsz, seq, N_BRANCHES, D_MODEL)
    merged = jnp.zeros_like(x)
    for i, y in enumerate((ya, yb, yc, yd)):
        y = y.reshape(bsz, seq, BRANCH_WIDTH).astype(x.dtype)
        merged = merged + jax.nn.sigmoid(gates[:, :, i]) * (y @ w_branch[i])
    return merged @ w_out


def swiglu(x, w_gate, w_up, w_down):
    return (jax.nn.silu(x @ w_gate) * (x @ w_up)) @ w_down


def routed_experts(xf, idx, wts, w_gate_e, w_up_e, w_down_e):
    n_tok, d = xf.shape
    nk = n_tok * TOP_K
    nblk = -(-nk // MOE_BLOCK) + N_EXPERTS
    rows = nblk * MOE_BLOCK
    flat_e = idx.reshape(-1)
    flat_tok = jnp.arange(nk, dtype=jnp.int32) // TOP_K
    flat_w = wts.reshape(-1)
    order = jnp.argsort(flat_e)
    se = flat_e[order]
    counts = jnp.bincount(flat_e, length=N_EXPERTS)
    starts = jnp.cumsum(counts) - counts
    padded = (counts + MOE_BLOCK - 1) // MOE_BLOCK * MOE_BLOCK
    pends = jnp.cumsum(padded)
    pstarts = pends - padded
    dest = pstarts[se] + jnp.arange(nk) - starts[se]
    row_tok = jnp.full((rows,), n_tok, jnp.int32).at[dest].set(flat_tok[order])
    row_w = jnp.zeros((rows,), jnp.float32).at[dest].set(flat_w[order])
    blk_e = jnp.minimum(jnp.searchsorted(pends, jnp.arange(nblk) * MOE_BLOCK, side='right'), N_EXPERTS - 1)
    x_pad = jnp.concatenate([xf, jnp.zeros((1, d), xf.dtype)], axis=0)

    def step(acc, blk):
        tok, w, e = blk
        yb = swiglu(x_pad[tok], w_gate_e[e], w_up_e[e], w_down_e[e]).astype(jnp.float32) * w[:, None]
        return acc.at[tok].add(yb), None

    acc, _ = lax.scan(step, jnp.zeros((n_tok + 1, d), jnp.float32),
                      (row_tok.reshape(nblk, MOE_BLOCK), row_w.reshape(nblk, MOE_BLOCK), blk_e))
    return acc[:n_tok].astype(xf.dtype)


def moe_sublayer(x, w_router, router_bias, w_gate_e, w_up_e, w_down_e, w_gate_s, w_up_s, w_down_s):
    bsz, seq, d = x.shape
    n_tok = bsz * seq
    xf = x.reshape(n_tok, d)
    scores = jax.nn.sigmoid((xf @ w_router).astype(jnp.float32))
    choice = scores + router_bias.astype(jnp.float32)
    group_score = jnp.sum(lax.top_k(choice.reshape(n_tok, N_GROUPS, -1), 2)[0], axis=-1)
    _, gidx = lax.top_k(group_score, TOPK_GROUPS)
    gmask = jnp.sum(jax.nn.one_hot(gidx, N_GROUPS, dtype=jnp.float32), axis=1) > 0
    emask = jnp.repeat(gmask, N_EXPERTS // N_GROUPS, axis=1)
    _, idx = lax.top_k(jnp.where(emask, choice, -jnp.inf), TOP_K)
    sel = jnp.take_along_axis(scores, idx, axis=1)
    wts = sel / jnp.sum(sel, axis=-1, keepdims=True) * ROUTED_SCALE
    routed = routed_experts(xf, idx, wts, w_gate_e, w_up_e, w_down_e)
    shared = swiglu(xf, w_gate_s, w_up_s, w_down_s)
    return (shared + routed).reshape(bsz, seq, d)


def setup_inputs(seed: int = 0) -> dict:
    key = jax.random.key(seed)
    ks = jax.random.split(key, 24)

    def nrm(k, shape, scale):
        return jax.random.normal(k, shape, jnp.float32) * scale

    L, D, E, F, FS = DEPTH, D_MODEL, N_EXPERTS, EXPERT_DIM, SHARED_DIM
    return {
        'x': nrm(ks[0], (BATCH, SEQ, D), 1.0),
        't5_table': nrm(ks[1], (T5_BUCKETS, T5_HEADS), 0.5),
        'w_in': nrm(ks[2], (L, D, D_IN), D ** -0.5),
        'b_forget': 2.0 + nrm(ks[3], (L, FOX_HEADS), 0.1),
        'attn_sinks': nrm(ks[4], (L, SWA_HEADS), 0.5),
        'lam_q1': nrm(ks[5], (L, DIFF_QK_DIM), 0.1),
        'lam_k1': nrm(ks[6], (L, DIFF_QK_DIM), 0.1),
        'lam_q2': nrm(ks[7], (L, DIFF_QK_DIM), 0.1),
        'lam_k2': nrm(ks[8], (L, DIFF_QK_DIM), 0.1),
        'diff_norm_g': 1.0 + nrm(ks[9], (L, HEAD_DIM), 0.05),
        'w_branch': nrm(ks[10], (L, N_BRANCHES, BRANCH_WIDTH, D), BETA * BRANCH_WIDTH ** -0.5),
        'w_out': nrm(ks[11], (L, D, D), BETA * D ** -0.5),
        'ln1_g': 1.0 + nrm(ks[12], (L, D), 0.05),
        'ln1_b': nrm(ks[13], (L, D), 0.02),
        'w_router': nrm(ks[14], (L, D, E), D ** -0.5),
        'router_bias': nrm(ks[15], (L, E), 0.01),
        'w_gate_e': nrm(ks[16], (L, E, D, F), D ** -0.5),
        'w_up_e': nrm(ks[17], (L, E, D, F), D ** -0.5),
        'w_down_e': nrm(ks[18], (L, E, F, D), BETA * F ** -0.5),
        'w_gate_s': nrm(ks[19], (L, D, FS), D ** -0.5),
        'w_up_s': nrm(ks[20], (L, D, FS), D ** -0.5),
        'w_down_s': nrm(ks[21], (L, FS, D), BETA * FS ** -0.5),
        'ln2_g': 1.0 + nrm(ks[22], (L, D), 0.05),
        'ln2_b': nrm(ks[23], (L, D), 0.02),
    }


def reference(x, t5_table, w_in, b_forget, attn_sinks, lam_q1, lam_k1, lam_q2, lam_k2,
              diff_norm_g, w_branch, w_out, ln1_g, ln1_b, w_router, router_bias,
              w_gate_e, w_up_e, w_down_e, w_gate_s, w_up_s, w_down_s, ln2_g, ln2_b):
    for layer in range(DEPTH):
        lam_init = 0.8 - 0.6 * math.exp(-0.3 * layer)
        mix = mixer_sublayer(x, t5_table, w_in[layer], b_forget[layer], attn_sinks[layer],
                             lam_q1[layer], lam_k1[layer], lam_q2[layer], lam_k2[layer],
                             diff_norm_g[layer], w_branch[layer], w_out[layer], lam_init)
        x = layer_norm(ALPHA * x + mix, ln1_g[layer], ln1_b[layer])
        ffn = moe_sublayer(x, w_router[layer], router_bias[layer], w_gate_e[layer], w_up_e[layer],
                           w_down_e[layer], w_gate_s[layer], w_up_s[layer], w_down_s[layer])
        x = layer_norm(ALPHA * x + ffn, ln2_g[layer], ln2_b[layer])
    return x
```

```python
import functools
import math

import numpy as np
import jax
import jax.numpy as jnp
from jax import lax
from jax.experimental import pallas as pl
from jax.experimental.pallas import tpu as pltpu

F32 = jnp.float32
BF16 = jnp.bfloat16
NEG_INF = float("-inf")

D_MODEL = 1024
DEPTH = 2
HEAD_DIM = 64
N_HEADS = 4
SWA_KV_HEADS = 2
SWA_WINDOW = 128
DIFF_QK_DIM = 32
DIL_PATTERNS = ((128, 1), (512, 4), (2048, 16))
BRANCH_WIDTH = N_HEADS * HEAD_DIM
BAND_BLOCK = 128
T5_BUCKETS = 32
T5_MAX_DIST = 2048
N_EXPERTS = 64
TOP_K = 8
N_GROUPS = 8
TOPK_GROUPS = 4
GROUP_SIZE = N_EXPERTS // N_GROUPS
EXPERT_DIM = 256
ROUTED_SCALE = 2.5
ALPHA = (2 * DEPTH) ** 0.25
LN_EPS = 1e-5

QKV_WIDTH = 11 * BRANCH_WIDTH
FLASH_TILE = 256
ROW_TILE = 512
EXPERT_TILE = 256
VMEM_LIMIT = 56 * 1024 * 1024


def _cparams(*sem):
    return pltpu.CompilerParams(dimension_semantics=sem, vmem_limit_bytes=VMEM_LIMIT)


def _nt_dot(a, b):
    return lax.dot_general(a, b, (((1,), (1,)), ((), ())), preferred_element_type=F32)


def _dot(a, b):
    return jnp.dot(a, b, preferred_element_type=F32)


def _sigmoid(x):
    return 1.0 / (1.0 + jnp.exp(-x))


def _layer_norm(z, g, b):
    mu = jnp.mean(z, axis=-1, keepdims=True)
    zc = z - mu
    var = jnp.mean(zc * zc, axis=-1, keepdims=True)
    return zc * lax.rsqrt(var + LN_EPS) * g + b


def _head_lanes(width, lo, size):
    lane = lax.broadcasted_iota(jnp.int32, (1, width), 1)
    return (lane >= lo) & (lane < lo + size)


def _proj_kernel(x_ref, w_ref, wf_ref, h_ref, f_ref):
    xb = x_ref[...].astype(BF16)
    h_ref[...] = _dot(xb, w_ref[...]).astype(BF16)
    f_ref[...] = _dot(xb, wf_ref[...])


def _proj(x2d, w_qkv, w_f):
    n = x2d.shape[0]
    return pl.pallas_call(
        _proj_kernel,
        grid=(n // ROW_TILE,),
        in_specs=[
            pl.BlockSpec((ROW_TILE, D_MODEL), lambda i: (i, 0)),
            pl.BlockSpec((D_MODEL, QKV_WIDTH), lambda i: (0, 0)),
            pl.BlockSpec((D_MODEL, 128), lambda i: (0, 0)),
        ],
        out_specs=[
            pl.BlockSpec((ROW_TILE, QKV_WIDTH), lambda i: (i, 0)),
            pl.BlockSpec((ROW_TILE, 128), lambda i: (i, 0)),
        ],
        out_shape=[
            jax.ShapeDtypeStruct((n, QKV_WIDTH), BF16),
            jax.ShapeDtypeStruct((n, 128), F32),
        ],
        compiler_params=_cparams("parallel"),
        name="proj",
    )(x2d, w_qkv, w_f)


def _online_update(s, m_ref, l_ref, idx):
    m_prev = m_ref[idx]
    m_new = jnp.maximum(m_prev, jnp.max(s, axis=-1, keepdims=True))
    a = jnp.exp(m_prev - m_new)
    p = jnp.exp(s - m_new)
    l_ref[idx] = a * l_ref[idx] + jnp.sum(p, axis=-1, keepdims=True)
    m_ref[idx] = m_new
    return p, a


def _per_head_full(cols, width=BRANCH_WIDTH):
    full = cols[-1]
    for h in range(len(cols) - 2, -1, -1):
        full = jnp.where(_head_lanes(width, h * HEAD_DIM, HEAD_DIM), cols[h], full)
    return full


def _fox_kernel(q_ref, k_ref, v_ref, cq_ref, ck_ref, o_ref, qm_ref, acc_ref, m_ref, l_ref):
    t = FLASH_TILE
    qi = pl.program_id(1)
    q = q_ref[0]
    for h in range(N_HEADS):
        qm_ref[h] = jnp.where(_head_lanes(BRANCH_WIDTH, h * HEAD_DIM, HEAD_DIM), q, jnp.zeros_like(q))
    m_ref[...] = jnp.full(m_ref.shape, NEG_INF, F32)
    l_ref[...] = jnp.zeros(l_ref.shape, F32)
    acc_ref[...] = jnp.zeros(acc_ref.shape, F32)
    cq = cq_ref[0]

    def step(ki, diagonal):
        start = pl.multiple_of(ki * t, t)
        k = k_ref[0, pl.ds(start, t), :]
        v = v_ref[0, pl.ds(start, t), :]
        ck = ck_ref[0, ki]
        pv = jnp.zeros((t, BRANCH_WIDTH), F32)
        rescale = []
        for h in range(N_HEADS):
            s = _nt_dot(qm_ref[h], k) + (cq[:, h:h + 1] - ck[h:h + 1, :])
            if diagonal:
                row = lax.broadcasted_iota(jnp.int32, (t, t), 0)
                col = lax.broadcasted_iota(jnp.int32, (t, t), 1)
                s = jnp.where(col <= row, s, NEG_INF)
            p, a = _online_update(s, m_ref, l_ref, h)
            vm = jnp.where(_head_lanes(BRANCH_WIDTH, h * HEAD_DIM, HEAD_DIM), v, jnp.zeros_like(v))
            pv = pv + _dot(p.astype(BF16), vm)
            rescale.append(a)
        acc_ref[...] = acc_ref[...] * _per_head_full(rescale) + pv

    def body(ki, carry):
        step(ki, False)
        return carry

    lax.fori_loop(0, qi, body, 0)
    step(qi, True)
    inv = _per_head_full([1.0 / l_ref[h] for h in range(N_HEADS)])
    o_ref[0] = (acc_ref[...] * inv).astype(o_ref.dtype)


def _fox(h3, cq, ck):
    bsz, seq, _ = h3.shape
    t = FLASH_TILE
    return pl.pallas_call(
        _fox_kernel,
        grid=(bsz, seq // t),
        in_specs=[
            pl.BlockSpec((1, t, BRANCH_WIDTH), lambda b, i: (b, i, 0)),
            pl.BlockSpec((1, seq, BRANCH_WIDTH), lambda b, i: (b, 0, 1)),
            pl.BlockSpec((1, seq, BRANCH_WIDTH), lambda b, i: (b, 0, 2)),
            pl.BlockSpec((1, t, N_HEADS), lambda b, i: (b, i, 0)),
            pl.BlockSpec((1, seq // t, 8, t), lambda b, i: (b, 0, 0, 0)),
        ],
        out_specs=pl.BlockSpec((1, t, BRANCH_WIDTH), lambda b, i: (b, i, 0)),
        out_shape=jax.ShapeDtypeStruct((bsz, seq, BRANCH_WIDTH), BF16),
        scratch_shapes=[
            pltpu.VMEM((N_HEADS, t, BRANCH_WIDTH), BF16),
            pltpu.VMEM((t, BRANCH_WIDTH), F32),
            pltpu.VMEM((N_HEADS, t, 1), F32),
            pltpu.VMEM((N_HEADS, t, 1), F32),
        ],
        compiler_params=_cparams("parallel", "arbitrary"),
        name="fox",
    )(h3, h3, h3, cq, ck)


def _diff_kernel(lam_ref, q_ref, k_ref, v_ref, bias_ref, g_ref, ones_ref, o_ref,
                 qm_ref, acc1_ref, acc2_ref, m_ref, l_ref, *, n_bias):
    t = FLASH_TILE
    qi = pl.program_id(1)
    q = q_ref[0]
    for h in range(N_HEADS):
        for j in range(2):
            lanes = _head_lanes(BRANCH_WIDTH, h * HEAD_DIM + j * DIFF_QK_DIM, DIFF_QK_DIM)
            qm_ref[2 * h + j] = jnp.where(lanes, q, jnp.zeros_like(q))
    m_ref[...] = jnp.full(m_ref.shape, NEG_INF, F32)
    l_ref[...] = jnp.zeros(l_ref.shape, F32)
    acc1_ref[...] = jnp.zeros(acc1_ref.shape, F32)
    acc2_ref[...] = jnp.zeros(acc2_ref.shape, F32)
    scale = DIFF_QK_DIM ** -0.5

    def step(ki, bias_idx):
        start = pl.multiple_of(ki * t, t)
        k = k_ref[0, pl.ds(start, t), :]
        v = v_ref[0, pl.ds(start, t), :]
        pv = [jnp.zeros((t, BRANCH_WIDTH), F32), jnp.zeros((t, BRANCH_WIDTH), F32)]
        rescale = [[], []]
        for h in range(N_HEADS):
            bias = bias_ref[bias_idx, h]
            vm = jnp.where(_head_lanes(BRANCH_WIDTH, h * HEAD_DIM, HEAD_DIM), v, jnp.zeros_like(v))
            for j in range(2):
                s = _nt_dot(qm_ref[2 * h + j], k) * scale + bias
                p, a = _online_update(s, m_ref, l_ref, 2 * h + j)
                pv[j] = pv[j] + _dot(p.astype(BF16), vm)
                rescale[j].append(a)
        acc1_ref[...] = acc1_ref[...] * _per_head_full(rescale[0]) + pv[0]
        acc2_ref[...] = acc2_ref[...] * _per_head_full(rescale[1]) + pv[1]

    def body(ki, carry):
        step(ki, jnp.minimum(qi - ki, n_bias - 1))
        return carry

    lax.fori_loop(0, qi, body, 0)
    step(qi, 0)
    inv1 = _per_head_full([1.0 / l_ref[2 * h] for h in range(N_HEADS)])
    inv2 = _per_head_full([1.0 / l_ref[2 * h + 1] for h in range(N_HEADS)])
    y = acc1_ref[...] * inv1 - lam_ref[0] * (acc2_ref[...] * inv2)
    y2 = y * y
    hi = y2.astype(BF16)
    lo = (y2 - hi.astype(F32)).astype(BF16)
    ss = _dot(hi, ones_ref[...]) + _dot(lo, ones_ref[...])
    o_ref[0] = (y * lax.rsqrt(ss * (1.0 / HEAD_DIM) + LN_EPS) * g_ref[...]).astype(o_ref.dtype)


def _diff(h3, lam, bias, g_full, ones_bd):
    bsz, seq, _ = h3.shape
    t = FLASH_TILE
    n_bias = bias.shape[0]
    return pl.pallas_call(
        functools.partial(_diff_kernel, n_bias=n_bias),
        grid=(bsz, seq // t),
        in_specs=[
            pl.BlockSpec(memory_space=pltpu.SMEM),
            pl.BlockSpec((1, t, BRANCH_WIDTH), lambda b, i: (b, i, 5)),
            pl.BlockSpec((1, seq, BRANCH_WIDTH), lambda b, i: (b, 0, 6)),
            pl.BlockSpec((1, seq, BRANCH_WIDTH), lambda b, i: (b, 0, 7)),
            pl.BlockSpec((n_bias, N_HEADS, t, t), lambda b, i: (0, 0, 0, 0)),
            pl.BlockSpec((1, BRANCH_WIDTH), lambda b, i: (0, 0)),
            pl.BlockSpec((BRANCH_WIDTH, BRANCH_WIDTH), lambda b, i: (0, 0)),
        ],
        out_specs=pl.BlockSpec((1, t, BRANCH_WIDTH), lambda b, i: (b, i, 0)),
        out_shape=jax.ShapeDtypeStruct((bsz, seq, BRANCH_WIDTH), BF16),
        scratch_shapes=[
            pltpu.VMEM((2 * N_HEADS, t, BRANCH_WIDTH), BF16),
            pltpu.VMEM((t, BRANCH_WIDTH), F32),
            pltpu.VMEM((t, BRANCH_WIDTH), F32),
            pltpu.VMEM((2 * N_HEADS, t, 1), F32),
            pltpu.VMEM((2 * N_HEADS, t, 1), F32),
        ],
        compiler_params=_cparams("parallel", "arbitrary"),
        name="diff",
    )(lam, h3, h3, h3, bias, g_full, ones_bd)


def _band_kernel(*refs, kvw, has_sinks, need_lse):
    refs = list(refs)
    sinks_ref = refs.pop(0) if has_sinks else None
    q_ref, k_ref, v_ref, bias_ref, o_ref = refs[:5]
    lse_ref = refs[5] if need_lse else None
    blk = BAND_BLOCK
    i = pl.program_id(2)
    start = pl.multiple_of(jnp.maximum(i - 1, 0) * blk, blk)
    tile = jnp.minimum(i, 1)
    q = q_ref[0]
    k = k_ref[0, pl.ds(start, 2 * blk), :]
    v = v_ref[0, pl.ds(start, 2 * blk), :]
    n_part = BRANCH_WIDTH // kvw
    outs = [jnp.zeros((blk, kvw), F32) for _ in range(n_part)]
    lses = []
    for p in range(N_HEADS):
        part, off = divmod(p * HEAD_DIM, kvw)
        lanes = _head_lanes(kvw, off, HEAD_DIM)
        qp = q[:, part * kvw:(part + 1) * kvw]
        s = _nt_dot(jnp.where(lanes, qp, jnp.zeros_like(qp)), k) + bias_ref[tile, p]
        m = jnp.max(s, axis=-1, keepdims=True)
        if has_sinks:
            m = jnp.maximum(m, sinks_ref[p])
        e = jnp.exp(s - m)
        den = jnp.sum(e, axis=-1, keepdims=True)
        if has_sinks:
            den = den + jnp.exp(sinks_ref[p] - m)
        vm = jnp.where(lanes, v, jnp.zeros_like(v))
        outs[part] = outs[part] + _dot(e.astype(BF16), vm) * (1.0 / den)
        if need_lse:
            lses.append(m + jnp.log(den))
    o = outs[0] if n_part == 1 else jnp.concatenate(outs, axis=-1)
    o_ref[0] = o.astype(o_ref.dtype)
    if need_lse:
        lse_ref[0] = _per_head_full(lses)


def _band(h3, bias, sinks, *, dil, q_col, k_col, v_col, kvw, out_dtype):
    bsz, seq, width = h3.shape
    sub = seq // dil
    blk = BAND_BLOCK
    hv = h3.reshape(bsz, sub, dil * width)
    qb, kb = width // BRANCH_WIDTH, width // kvw
    has_sinks = sinks is not None
    need_lse = not has_sinks
    in_specs = [
        pl.BlockSpec((1, blk, BRANCH_WIDTH), lambda b, r, i: (b, i, r * qb + q_col)),
        pl.BlockSpec((1, sub, kvw), lambda b, r, i: (b, 0, r * kb + k_col)),
        pl.BlockSpec((1, sub, kvw), lambda b, r, i: (b, 0, r * kb + v_col)),
        pl.BlockSpec((2, N_HEADS, blk, 2 * blk), lambda b, r, i: (0, 0, 0, 0)),
    ]
    args = [hv, hv, hv, bias]
    if has_sinks:
        in_specs.insert(0, pl.BlockSpec(memory_space=pltpu.SMEM))
        args.insert(0, sinks)
    o_spec = pl.BlockSpec((1, blk, BRANCH_WIDTH), lambda b, r, i: (b, i, r))
    o_shape = jax.ShapeDtypeStruct((bsz, sub, dil * BRANCH_WIDTH), out_dtype)
    out_specs, out_shape = [o_spec], [o_shape]
    if need_lse:
        out_specs.append(o_spec)
        out_shape.append(jax.ShapeDtypeStruct((bsz, sub, dil * BRANCH_WIDTH), F32))
    res = pl.pallas_call(
        functools.partial(_band_kernel, kvw=kvw, has_sinks=has_sinks, need_lse=need_lse),
        grid=(bsz, dil, sub // blk),
        in_specs=in_specs,
        out_specs=out_specs,
        out_shape=out_shape,
        compiler_params=_cparams("parallel", "parallel", "arbitrary"),
        name=f"band_d{dil}" + ("_sink" if has_sinks else ""),
    )(*args)
    return [r.reshape(bsz * seq, BRANCH_WIDTH) for r in res]


def _merge_kernel(x_ref, ya_ref, yb_ref, yc_ref, o1_ref, o2_ref, o3_ref, l1_ref, l2_ref, l3_ref,
                  wg_ref, wb_ref, wo_ref, g_ref, b_ref, out_ref, outb_ref):
    x = x_ref[...]
    xb = x.astype(BF16)
    l1, l2, l3 = l1_ref[...], l2_ref[...], l3_ref[...]
    mx = jnp.maximum(jnp.maximum(l1, l2), l3)
    e1, e2, e3 = jnp.exp(l1 - mx), jnp.exp(l2 - mx), jnp.exp(l3 - mx)
    yd = (e1 * o1_ref[...] + e2 * o2_ref[...] + e3 * o3_ref[...]) / (e1 + e2 + e3)
    ys = [ya_ref[...], yb_ref[...], yc_ref[...], yd.astype(BF16)]
    merged = jnp.zeros((x.shape[0], D_MODEL), F32)
    for i in range(4):
        gate = _dot(xb, wg_ref[:, i * D_MODEL:(i + 1) * D_MODEL])
        merged = merged + _sigmoid(gate) * _dot(ys[i], wb_ref[i])
    mix = _dot(merged.astype(BF16), wo_ref[...])
    out = _layer_norm(ALPHA * x + mix, g_ref[...], b_ref[...])
    out_ref[...] = out
    outb_ref[...] = out.astype(BF16)


def _merge(x2d, ya, yb, yc, od, ld, w_gates, w_branch, w_out, ln_g, ln_b):
    n = x2d.shape[0]
    tm = ROW_TILE
    row = lambda w: pl.BlockSpec((tm, w), lambda i: (i, 0))
    full = lambda *shape: pl.BlockSpec(shape, lambda i: (0,) * len(shape))
    return pl.pallas_call(
        _merge_kernel,
        grid=(n // tm,),
        in_specs=[row(D_MODEL)] + [row(BRANCH_WIDTH)] * 9 + [
            full(D_MODEL, 4 * D_MODEL), full(4, BRANCH_WIDTH, D_MODEL), full(D_MODEL, D_MODEL),
            full(1, D_MODEL), full(1, D_MODEL)],
        out_specs=[row(D_MODEL), row(D_MODEL)],
        out_shape=[jax.ShapeDtypeStruct((n, D_MODEL), F32), jax.ShapeDtypeStruct((n, D_MODEL), BF16)],
        compiler_params=_cparams("parallel"),
        name="merge",
    )(x2d, ya, yb, yc, *od, *ld, w_gates, w_branch, w_out, ln_g, ln_b)


def _row_iota(rows, cols):
    return lax.broadcasted_iota(jnp.int32, (rows, cols), 0).astype(F32)


def _first_max(vals, iota, size):
    m = jnp.max(vals, axis=0, keepdims=True)
    idx = jnp.min(jnp.where(vals == m, iota, float(size)), axis=0, keepdims=True)
    return m, idx


def _router_kernel(x_ref, wr_ref, rb_ref, idx_ref, w_ref):
    tm = x_ref.shape[0]
    logits = _nt_dot(wr_ref[...], x_ref[...])
    scores = _sigmoid(logits)
    choice = scores + rb_ref[...]
    g_iota = _row_iota(GROUP_SIZE, tm)
    group_rows = []
    for g in range(N_GROUPS):
        c = choice[g * GROUP_SIZE:(g + 1) * GROUP_SIZE, :]
        m1, i1 = _first_max(c, g_iota, GROUP_SIZE)
        m2 = jnp.max(jnp.where(g_iota == i1, NEG_INF, c), axis=0, keepdims=True)
        group_rows.append(m1 + m2)
    gs = jnp.concatenate(group_rows, axis=0)
    n_iota = _row_iota(N_GROUPS, tm)
    picked = jnp.zeros((N_GROUPS, tm), F32)
    for _ in range(TOPK_GROUPS):
        _, gi = _first_max(gs, n_iota, N_GROUPS)
        hit = n_iota == gi
        picked = jnp.where(hit, 1.0, picked)
        gs = jnp.where(hit, NEG_INF, gs)
    masked = jnp.concatenate(
        [jnp.where(picked[g:g + 1, :] > 0.0, choice[g * GROUP_SIZE:(g + 1) * GROUP_SIZE, :], NEG_INF)
         for g in range(N_GROUPS)], axis=0)
    e_iota = _row_iota(N_EXPERTS, tm)
    idxs, sels = [], []
    for _ in range(TOP_K):
        _, ei = _first_max(masked, e_iota, N_EXPERTS)
        hit = e_iota == ei
        idxs.append(ei)
        sels.append(jnp.sum(jnp.where(hit, scores, 0.0), axis=0, keepdims=True))
        masked = jnp.where(hit, NEG_INF, masked)
    sel = jnp.concatenate(sels, axis=0)
    idx_ref[...] = jnp.concatenate(idxs, axis=0).astype(jnp.int32)
    w_ref[...] = sel / jnp.sum(sel, axis=0, keepdims=True) * ROUTED_SCALE


def _router(x1b, w_router_t, router_bias):
    n = x1b.shape[0]
    tm = ROW_TILE
    return pl.pallas_call(
        _router_kernel,
        grid=(n // tm,),
        in_specs=[
            pl.BlockSpec((tm, D_MODEL), lambda i: (i, 0)),
            pl.BlockSpec((N_EXPERTS, D_MODEL), lambda i: (0, 0)),
            pl.BlockSpec((N_EXPERTS, 1), lambda i: (0, 0)),
        ],
        out_specs=[pl.BlockSpec((TOP_K, tm), lambda i: (0, i))] * 2,
        out_shape=[jax.ShapeDtypeStruct((TOP_K, n), jnp.int32), jax.ShapeDtypeStruct((TOP_K, n), F32)],
        compiler_params=_cparams("parallel"),
        name="router",
    )(x1b, w_router_t, router_bias)


def _expert_kernel(blk_e_ref, n_used_ref, xs_ref, rw_ref, wg_ref, wu_ref, wd_ref, y_ref):
    i = pl.program_id(0)

    @pl.when(i < n_used_ref[0])
    def _():
        xs = xs_ref[...]
        gate = _dot(xs, wg_ref[0])
        up = _dot(xs, wu_ref[0])
        hidden = (gate * _sigmoid(gate) * up).astype(BF16)
        y_ref[...] = _dot(hidden, wd_ref[0]) * rw_ref[...]

    @pl.when(i >= n_used_ref[0])
    def _():
        y_ref[...] = jnp.zeros(y_ref.shape, y_ref.dtype)


def _experts(blk_e, n_used, xs, row_w, w_gate_e, w_up_e, w_down_e):
    rows = xs.shape[0]
    tb = EXPERT_TILE
    grid_spec = pltpu.PrefetchScalarGridSpec(
        num_scalar_prefetch=2,
        grid=(rows // tb,),
        in_specs=[
            pl.BlockSpec((tb, D_MODEL), lambda i, be, nu: (i, 0)),
            pl.BlockSpec((tb, 1), lambda i, be, nu: (i, 0)),
            pl.BlockSpec((1, D_MODEL, EXPERT_DIM), lambda i, be, nu: (be[i], 0, 0)),
            pl.BlockSpec((1, D_MODEL, EXPERT_DIM), lambda i, be, nu: (be[i], 0, 0)),
            pl.BlockSpec((1, EXPERT_DIM, D_MODEL), lambda i, be, nu: (be[i], 0, 0)),
        ],
        out_specs=pl.BlockSpec((tb, D_MODEL), lambda i, be, nu: (i, 0)),
    )
    return pl.pallas_call(
        _expert_kernel,
        grid_spec=grid_spec,
        out_shape=jax.ShapeDtypeStruct((rows, D_MODEL), F32),
        compiler_params=_cparams("arbitrary"),
        name="experts",
    )(blk_e, n_used, xs, row_w, w_gate_e, w_up_e, w_down_e)


def _ffn_kernel(x_ref, xb_ref, r_ref, wg_ref, wu_ref, wd_ref, g_ref, b_ref, out_ref):
    xb = xb_ref[...]
    gate = _dot(xb, wg_ref[...])
    up = _dot(xb, wu_ref[...])
    shared = _dot((gate * _sigmoid(gate) * up).astype(BF16), wd_ref[...])
    out_ref[...] = _layer_norm(ALPHA * x_ref[...] + (shared + r_ref[...]), g_ref[...], b_ref[...])


def _ffn(x1, x1b, routed, w_gate_s, w_up_s, w_down_s, ln_g, ln_b):
    n = x1.shape[0]
    tm = ROW_TILE
    row = pl.BlockSpec((tm, D_MODEL), lambda i: (i, 0))
    full = lambda *shape: pl.BlockSpec(shape, lambda i: (0,) * len(shape))
    return pl.pallas_call(
        _ffn_kernel,
        grid=(n // tm,),
        in_specs=[row, row, row, full(D_MODEL, EXPERT_DIM), full(D_MODEL, EXPERT_DIM),
                  full(EXPERT_DIM, D_MODEL), full(1, D_MODEL), full(1, D_MODEL)],
        out_specs=row,
        out_shape=jax.ShapeDtypeStruct((n, D_MODEL), F32),
        compiler_params=_cparams("parallel"),
        name="ffn",
    )(x1, x1b, routed, w_gate_s, w_up_s, w_down_s, ln_g, ln_b)


def _t5_bucket(dist):
    n = jnp.maximum(dist, 0)
    exact = T5_BUCKETS // 2
    nf = jnp.maximum(n, 1).astype(F32)
    large = exact + (jnp.log(nf / exact) / math.log(T5_MAX_DIST / exact) * (T5_BUCKETS - exact)).astype(jnp.int32)
    large = jnp.minimum(large, T5_BUCKETS - 1)
    return jnp.where(n < exact, n, large)


def _bias_by_distance(table, max_dist):
    return table[_t5_bucket(jnp.arange(max_dist + 1, dtype=jnp.int32))].T


def _toeplitz_tiles(bias_vec, dist, valid):
    safe = np.where(valid, dist, 0).astype(np.int32)
    tiles = bias_vec[:, safe]
    tiles = jnp.where(valid[None], tiles, NEG_INF)
    return jnp.moveaxis(tiles, 0, -3)


def _band_bias(table, max_dist, dil):
    blk = BAND_BLOCK
    r = np.arange(blk)[:, None]
    j = np.arange(2 * blk)[None, :]
    dist = np.stack([r - j, blk + r - j])
    valid = (dist >= 0) & (dist <= max_dist)
    vec = _bias_by_distance(table, max_dist * dil)[:, ::dil]
    return _toeplitz_tiles(vec, dist, valid)


def _diff_bias(table, seq):
    t = FLASH_TILE
    saturation = int(math.ceil((T5_BUCKETS // 2) * (T5_MAX_DIST / (T5_BUCKETS // 2)) ** ((T5_BUCKETS // 2 - 1) / (T5_BUCKETS // 2)) * 1.05))
    n_bias = min(seq // t, -(-(saturation - 1) // t) + 2)
    r = np.arange(t)[:, None]
    c = np.arange(t)[None, :]
    dist = np.stack([d * t + r - c for d in range(n_bias)])
    valid = dist >= 0
    vec = _bias_by_distance(table, min(seq, n_bias * t))
    return _toeplitz_tiles(vec, dist, valid)


SWA_HEAD_ORDER = (0, 2, 1, 3)


def _pack_qkv_weights(w_in):
    offs = np.cumsum([0, 256, 256, 256, 4, 256, 128, 128, 256, 256, 256, 256, 256, 256])
    (qa, ka, va, fa, qb, kb, vb, qc, kc, vc, qd, kd, vd) = [int(o) for o in offs[:13]]
    gates = int(offs[13])
    col = lambda o, w: w_in[:, o:o + w]
    scale = HEAD_DIM ** -0.5
    qb_w = col(qb, 256).reshape(D_MODEL, N_HEADS, HEAD_DIM)[:, np.asarray(SWA_HEAD_ORDER)].reshape(D_MODEL, 256)
    w_qkv = jnp.concatenate([
        col(qa, 256) * scale, col(ka, 256), col(va, 256),
        qb_w * scale, col(kb, 128), col(vb, 128),
        col(qc, 256), col(kc, 256), col(vc, 256),
        col(qd, 256) * scale, col(kd, 256), col(vd, 256)], axis=1).astype(BF16)
    w_f = jnp.pad(col(fa, 4), ((0, 0), (0, 124))).astype(BF16)
    w_gates = col(gates, 4 * D_MODEL).astype(BF16)
    return w_qkv, w_f, w_gates


def _dispatch(idx_t, wts_t, n_tok):
    tb = EXPERT_TILE
    nk = n_tok * TOP_K
    nblk = nk // tb + N_EXPERTS
    rows = nblk * tb
    flat_e = idx_t.T.reshape(-1)
    flat_w = wts_t.T.reshape(-1)
    order = jnp.argsort(flat_e)
    se = flat_e[order]
    counts = jnp.bincount(flat_e, length=N_EXPERTS)
    starts = jnp.cumsum(counts) - counts
    padded = (counts + tb - 1) // tb * tb
    pends = jnp.cumsum(padded)
    pstarts = pends - padded
    dest = (pstarts[se] + jnp.arange(nk) - starts[se]).astype(jnp.int32)
    row_tok = jnp.full((rows,), n_tok, jnp.int32).at[dest].set((order // TOP_K).astype(jnp.int32))
    row_w = jnp.zeros((rows,), F32).at[dest].set(flat_w[order])
    blk_e = jnp.minimum(jnp.searchsorted(pends, jnp.arange(nblk) * tb, side="right"), N_EXPERTS - 1).astype(jnp.int32)
    n_used = (pends[-1] // tb).astype(jnp.int32).reshape(1)
    slot = jnp.zeros((nk,), jnp.int32).at[order].set(dest)
    return row_tok, row_w, blk_e, n_used, slot


def kernel(x, t5_table, w_in, b_forget, attn_sinks, lam_q1, lam_k1, lam_q2, lam_k2, diff_norm_g, w_branch, w_out, ln1_g, ln1_b, w_router, router_bias, w_gate_e, w_up_e, w_down_e, w_gate_s, w_up_s, w_down_s, ln2_g, ln2_b):
    bsz, seq, d = x.shape
    n_tok = bsz * seq
    t = FLASH_TILE
    table = t5_table.astype(F32)
    swa_bias = _band_bias(table[:, np.asarray(SWA_HEAD_ORDER)], SWA_WINDOW - 1, 1)
    diff_bias = _diff_bias(table[:, N_HEADS:2 * N_HEADS], seq)
    dil_bias = [_band_bias(table[:, 2 * N_HEADS:], window // dil, dil) for window, dil in DIL_PATTERNS]
    ones_bd = jnp.asarray(np.kron(np.eye(N_HEADS), np.ones((HEAD_DIM, HEAD_DIM))), BF16)

    x2d = x.reshape(n_tok, d)
    for layer in range(DEPTH):
        lam_init = 0.8 - 0.6 * math.exp(-0.3 * layer)
        w_qkv, w_f, w_gates = _pack_qkv_weights(w_in[layer])
        h, f_logit = _proj(x2d, w_qkv, w_f)
        h3 = h.reshape(bsz, seq, QKV_WIDTH)

        log_f = jax.nn.log_sigmoid(f_logit[:, :N_HEADS] + b_forget[layer].astype(F32))
        c = jnp.cumsum(log_f.reshape(bsz, seq, N_HEADS), axis=1)
        ck = jnp.pad(c.reshape(bsz, seq // t, t, N_HEADS).transpose(0, 1, 3, 2), ((0, 0), (0, 0), (0, 8 - N_HEADS), (0, 0)))
        ya = _fox(h3, c, ck).reshape(n_tok, BRANCH_WIDTH)

        sinks = attn_sinks[layer].astype(F32)[jnp.asarray(SWA_HEAD_ORDER)]
        (yb,) = _band(h3, swa_bias, sinks, dil=1, q_col=3, k_col=8, v_col=9, kvw=128, out_dtype=BF16)

        lam = (jnp.exp(jnp.sum(lam_q1[layer].astype(F32) * lam_k1[layer].astype(F32)))
               - jnp.exp(jnp.sum(lam_q2[layer].astype(F32) * lam_k2[layer].astype(F32))) + lam_init)
        g_full = (jnp.tile(diff_norm_g[layer].astype(F32), N_HEADS) * (1.0 - lam_init)).reshape(1, BRANCH_WIDTH)
        yc = _diff(h3, lam.reshape(1), diff_bias, g_full, ones_bd).reshape(n_tok, BRANCH_WIDTH)

        od, ld = [], []
        for (window, dil), bias in zip(DIL_PATTERNS, dil_bias):
            o, lse = _band(h3, bias, None, dil=dil, q_col=8, k_col=9, v_col=10, kvw=256, out_dtype=F32)
            od.append(o)
            ld.append(lse)

        wb = w_branch[layer]
        wb = jnp.stack([wb[0], wb[1].reshape(N_HEADS, HEAD_DIM, d)[np.asarray(SWA_HEAD_ORDER)].reshape(BRANCH_WIDTH, d),
                        wb[2], wb[3]]).astype(BF16)
        x1, x1b = _merge(x2d, ya, yb, yc, od, ld, w_gates, wb, w_out[layer].astype(BF16),
                         ln1_g[layer].reshape(1, d), ln1_b[layer].reshape(1, d))

        idx_t, wts_t = _router(x1b, w_router[layer].T.astype(BF16), router_bias[layer].astype(F32).reshape(N_EXPERTS, 1))
        row_tok, row_w, blk_e, n_used, slot = _dispatch(idx_t, wts_t, n_tok)
        xs = jnp.concatenate([x1b, jnp.zeros((1, d), BF16)], axis=0)[row_tok]
        y = _experts(blk_e, n_used, xs, row_w[:, None], w_gate_e[layer].astype(BF16),
                     w_up_e[layer].astype(BF16), w_down_e[layer].astype(BF16))
        routed = y[slot].reshape(n_tok, TOP_K, d).sum(axis=1)

        x2d = _ffn(x1, x1b, routed, w_gate_s[layer].astype(BF16), w_up_s[layer].astype(BF16),
                   w_down_s[layer].astype(BF16), ln2_g[layer].reshape(1, d), ln2_b[layer].reshape(1, d))
    return x2d.reshape(bsz, seq, d)
```

```python
import functools
import math

import numpy as np
import jax
import jax.numpy as jnp
from jax import lax
from jax.experimental import pallas as pl
from jax.experimental.pallas import tpu as pltpu

F32 = jnp.float32
BF16 = jnp.bfloat16
NEG_INF = float("-inf")

D_MODEL = 1024
DEPTH = 2
HEAD_DIM = 64
N_HEADS = 4
SWA_KV_HEADS = 2
SWA_WINDOW = 128
DIFF_QK_DIM = 32
DIL_PATTERNS = ((128, 1), (512, 4), (2048, 16))
BRANCH_WIDTH = N_HEADS * HEAD_DIM
BAND_BLOCK = 128
T5_BUCKETS = 32
T5_MAX_DIST = 2048
N_EXPERTS = 64
TOP_K = 8
N_GROUPS = 8
TOPK_GROUPS = 4
GROUP_SIZE = N_EXPERTS // N_GROUPS
EXPERT_DIM = 256
ROUTED_SCALE = 2.5
ALPHA = (2 * DEPTH) ** 0.25
LN_EPS = 1e-5
LOG2_E = math.log2(math.e)

ROWS_WIDTH = 7 * BRANCH_WIDTH
FLASH_TILE = 256
KV_TILE = 128
KV_PER_Q = FLASH_TILE // KV_TILE
ROW_TILE = 512
EXPERT_TILE = 256
VMEM_LIMIT = 56 * 1024 * 1024


def _cparams(*sem):
    return pltpu.CompilerParams(dimension_semantics=sem, vmem_limit_bytes=VMEM_LIMIT)


def _nt_dot(a, b):
    return lax.dot_general(a, b, (((1,), (1,)), ((), ())), preferred_element_type=F32)


def _dot(a, b):
    return jnp.dot(a, b, preferred_element_type=F32)


def _sigmoid(x):
    return 1.0 / (1.0 + jnp.exp(-x))


def _layer_norm(z, g, b):
    mu = jnp.mean(z, axis=-1, keepdims=True)
    zc = z - mu
    var = jnp.mean(zc * zc, axis=-1, keepdims=True)
    return zc * lax.rsqrt(var + LN_EPS) * g + b


def _head_lanes(width, lo, size):
    lane = lax.broadcasted_iota(jnp.int32, (1, width), 1)
    return (lane >= lo) & (lane < lo + size)


def _proj_kernel(x_ref, w_ref, wt_ref, wf_ref, h_ref, qta_ref, vta_ref, qtc_ref, vtc_ref, f_ref):
    t = KV_TILE
    xb = x_ref[0].astype(BF16)
    h_ref[0] = _dot(xb, w_ref[...]).astype(BF16)
    f_ref[0] = _dot(xb, wf_ref[...])
    tt = _nt_dot(wt_ref[...], xb)
    w = BRANCH_WIDTH
    qta_ref[0] = tt[0:w].astype(BF16)
    qtc_ref[0] = tt[2 * w:3 * w].astype(BF16)
    for j in range(xb.shape[0] // t):
        vta_ref[0, j] = tt[w:2 * w, j * t:(j + 1) * t].astype(BF16)
        vtc_ref[0, j] = tt[3 * w:4 * w, j * t:(j + 1) * t].astype(BF16)


def _proj(x3, w_rows, w_t, w_f):
    bsz, seq, d = x3.shape
    tm, t, w = ROW_TILE, KV_TILE, BRANCH_WIDTH
    qt_spec = pl.BlockSpec((1, w, tm), lambda b, i: (b, 0, i))
    vt_spec = pl.BlockSpec((1, tm // t, w, t), lambda b, i: (b, i, 0, 0))
    qt_shape = jax.ShapeDtypeStruct((bsz, w, seq), BF16)
    vt_shape = jax.ShapeDtypeStruct((bsz, seq // t, w, t), BF16)
    return pl.pallas_call(
        _proj_kernel,
        grid=(bsz, seq // tm),
        in_specs=[
            pl.BlockSpec((1, tm, d), lambda b, i: (b, i, 0)),
            pl.BlockSpec((d, ROWS_WIDTH), lambda b, i: (0, 0)),
            pl.BlockSpec((4 * w, d), lambda b, i: (0, 0)),
            pl.BlockSpec((d, 128), lambda b, i: (0, 0)),
        ],
        out_specs=[
            pl.BlockSpec((1, tm, ROWS_WIDTH), lambda b, i: (b, i, 0)),
            qt_spec, vt_spec, qt_spec, vt_spec,
            pl.BlockSpec((1, tm, 128), lambda b, i: (b, i, 0)),
        ],
        out_shape=[
            jax.ShapeDtypeStruct((bsz, seq, ROWS_WIDTH), BF16),
            qt_shape, vt_shape, qt_shape, vt_shape,
            jax.ShapeDtypeStruct((bsz, seq, 128), F32),
        ],
        compiler_params=_cparams("parallel", "parallel"),
        name="proj",
    )(x3, w_rows, w_t, w_f)


def _online_update(s, m_ref, l_ref, idx, exp=jnp.exp):
    row = slice(idx, idx + 1)
    m_prev = m_ref[row, :]
    m_new = jnp.maximum(m_prev, jnp.max(s, axis=0, keepdims=True))
    a = exp(m_prev - m_new)
    p = exp(s - m_new)
    l_ref[row, :] = a * l_ref[row, :] + jnp.sum(p, axis=0, keepdims=True)
    m_ref[row, :] = m_new
    return p, a


def _band_rows(x, lo, size):
    rows, cols = x.shape
    parts = []
    if lo:
        parts.append(jnp.zeros((lo, cols), x.dtype))
    parts.append(x[lo:lo + size])
    if lo + size < rows:
        parts.append(jnp.zeros((rows - lo - size, cols), x.dtype))
    return jnp.concatenate(parts, axis=0)


def _per_head_full(cols, width=BRANCH_WIDTH):
    full = cols[-1]
    for h in range(len(cols) - 2, -1, -1):
        full = jnp.where(_head_lanes(width, h * HEAD_DIM, HEAD_DIM), cols[h], full)
    return full


def _flash_init(m_ref, l_ref, *acc_refs):
    m_ref[...] = jnp.full(m_ref.shape, NEG_INF, F32)
    l_ref[...] = jnp.zeros(l_ref.shape, F32)
    for acc_ref in acc_refs:
        acc_ref[...] = jnp.zeros(acc_ref.shape, F32)


def _fox_kernel(qt_ref, k_ref, vt_ref, cq_ref, ck_ref, o_ref, qm_ref, acc_ref, m_ref, l_ref):
    t, tk = FLASH_TILE, KV_TILE
    qi = pl.program_id(1)
    qt = qt_ref[0]
    for h in range(N_HEADS):
        qm_ref[h] = _band_rows(qt[128 * (h // 2):128 * (h // 2 + 1)], HEAD_DIM * (h % 2), HEAD_DIM)
    _flash_init(m_ref, l_ref, acc_ref)
    cq = cq_ref[0]

    def step(ki, diagonal):
        start = pl.multiple_of(ki * tk, tk)
        k = k_ref[0, pl.ds(start, tk), :]
        ck = ck_ref[0, pl.ds(start, tk), :]
        qk = lambda h: _dot(k[:, 128 * (h // 2):128 * (h // 2 + 1)], qm_ref[h])
        ahead = 2
        pending = [qk(h) for h in range(ahead)]
        for h in range(N_HEADS):
            rows = slice(h * HEAD_DIM, (h + 1) * HEAD_DIM)
            s = pending.pop(0)
            if h + ahead < N_HEADS:
                pending.append(qk(h + ahead))
            s = s + (cq[h:h + 1, :] - ck[:, h:h + 1])
            if diagonal is not None:
                key = lax.broadcasted_iota(jnp.int32, (tk, t), 0) + diagonal * tk
                query = lax.broadcasted_iota(jnp.int32, (tk, t), 1)
                s = jnp.where(key <= query, s, NEG_INF)
            p, a = _online_update(s, m_ref, l_ref, h)
            acc_ref[rows, :] = a * acc_ref[rows, :] + _dot(vt_ref[0, ki, rows, :], p.astype(BF16))

    def body(ki, carry):
        step(ki, None)
        return carry

    lax.fori_loop(0, qi * KV_PER_Q, body, 0)
    for j in range(KV_PER_Q):
        step(qi * KV_PER_Q + j, j)
    yt = jnp.concatenate([acc_ref[h * HEAD_DIM:(h + 1) * HEAD_DIM, :] * (1.0 / l_ref[h:h + 1, :])
                          for h in range(N_HEADS)], axis=0)
    o_ref[0] = yt.T.astype(o_ref.dtype)


def _flash_specs(bsz, seq, k_col):
    t, w = FLASH_TILE, BRANCH_WIDTH
    in_specs = [
        pl.BlockSpec((1, w, t), lambda b, i: (b, 0, i)),
        pl.BlockSpec((1, seq, w), lambda b, i: (b, 0, k_col)),
        pl.BlockSpec((1, seq // KV_TILE, w, KV_TILE), lambda b, i: (b, 0, 0, 0)),
    ]
    out_spec = pl.BlockSpec((1, t, w), lambda b, i: (b, i, 0))
    out_shape = jax.ShapeDtypeStruct((bsz, seq, w), BF16)
    return in_specs, out_spec, out_shape


def _fox(qt, h3, vt, cq, ck):
    bsz, seq, _ = h3.shape
    t, w = FLASH_TILE, BRANCH_WIDTH
    in_specs, out_spec, out_shape = _flash_specs(bsz, seq, 0)
    return pl.pallas_call(
        _fox_kernel,
        grid=(bsz, seq // t),
        in_specs=in_specs + [
            pl.BlockSpec((1, 8, t), lambda b, i: (b, 0, i)),
            pl.BlockSpec((1, seq, N_HEADS), lambda b, i: (b, 0, 0)),
        ],
        out_specs=out_spec,
        out_shape=out_shape,
        scratch_shapes=[
            pltpu.VMEM((N_HEADS, 128, t), BF16),
            pltpu.VMEM((w, t), F32),
            pltpu.VMEM((8, t), F32),
            pltpu.VMEM((8, t), F32),
        ],
        compiler_params=_cparams("parallel", "arbitrary"),
        name="fox",
    )(qt, h3, vt, cq, ck)


def _diff_kernel(lam_ref, qt_ref, k_ref, vt_ref, bias_ref, g_ref, o_ref,
                 qm_ref, acc1_ref, acc2_ref, m_ref, l_ref, *, n_bias):
    t, tk = FLASH_TILE, KV_TILE
    qi = pl.program_id(1)
    qt = qt_ref[0]
    for h in range(N_HEADS):
        for j in range(2):
            qm_ref[2 * h + j] = _band_rows(qt[128 * (h // 2):128 * (h // 2 + 1)],
                                           HEAD_DIM * (h % 2) + DIFF_QK_DIM * j, DIFF_QK_DIM)
    _flash_init(m_ref, l_ref, acc1_ref, acc2_ref)
    accs = (acc1_ref, acc2_ref)
    scale = DIFF_QK_DIM ** -0.5 * LOG2_E

    def body(ki, carry):
        bias_idx = jnp.minimum(qi * KV_PER_Q + (KV_PER_Q - 1) - ki, n_bias - 1)
        start = pl.multiple_of(ki * tk, tk)
        k = k_ref[0, pl.ds(start, tk), :]
        qk = lambda i: _dot(k[:, 128 * (i // 4):128 * (i // 4 + 1)], qm_ref[i])
        ahead = 3
        pending = [qk(i) for i in range(ahead)]
        for i in range(2 * N_HEADS):
            h, j = divmod(i, 2)
            rows = slice(h * HEAD_DIM, (h + 1) * HEAD_DIM)
            s = pending.pop(0)
            if i + ahead < 2 * N_HEADS:
                pending.append(qk(i + ahead))
            s = s * scale + bias_ref[bias_idx, h]
            p, a = _online_update(s, m_ref, l_ref, i, exp=jnp.exp2)
            accs[j][rows, :] = a * accs[j][rows, :] + _dot(vt_ref[0, ki, rows, :], p.astype(BF16))
        return carry

    lax.fori_loop(0, (qi + 1) * KV_PER_Q, body, 0)
    outs = []
    for h in range(N_HEADS):
        rows = slice(h * HEAD_DIM, (h + 1) * HEAD_DIM)
        y = (acc1_ref[rows, :] * (1.0 / l_ref[2 * h:2 * h + 1, :])
             - lam_ref[0] * (acc2_ref[rows, :] * (1.0 / l_ref[2 * h + 1:2 * h + 2, :])))
        ms = jnp.mean(y * y, axis=0, keepdims=True)
        outs.append(y * lax.rsqrt(ms + LN_EPS) * g_ref[rows, :])
    o_ref[0] = jnp.concatenate(outs, axis=0).T.astype(o_ref.dtype)


def _diff(qt, h3, vt, lam, bias, g_col):
    bsz, seq, _ = h3.shape
    t, w = FLASH_TILE, BRANCH_WIDTH
    n_bias = bias.shape[0]
    in_specs, out_spec, out_shape = _flash_specs(bsz, seq, 3)
    return pl.pallas_call(
        functools.partial(_diff_kernel, n_bias=n_bias),
        grid=(bsz, seq // t),
        in_specs=[pl.BlockSpec(memory_space=pltpu.SMEM)] + in_specs + [
            pl.BlockSpec((n_bias, N_HEADS, KV_TILE, t), lambda b, i: (0, 0, 0, 0)),
            pl.BlockSpec((w, 1), lambda b, i: (0, 0)),
        ],
        out_specs=out_spec,
        out_shape=out_shape,
        scratch_shapes=[
            pltpu.VMEM((2 * N_HEADS, 128, t), BF16),
            pltpu.VMEM((w, t), F32),
            pltpu.VMEM((w, t), F32),
            pltpu.VMEM((8, t), F32),
            pltpu.VMEM((8, t), F32),
        ],
        compiler_params=_cparams("parallel", "arbitrary"),
        name="diff",
    )(lam, qt, h3, vt, bias, g_col)


def _band_kernel(*refs, kvw, has_sinks, need_lse):
    refs = list(refs)
    sinks_ref = refs.pop(0) if has_sinks else None
    q_ref, k_ref, v_ref, bias_ref, o_ref = refs[:5]
    lse_ref = refs[5] if need_lse else None
    blk = BAND_BLOCK
    i = pl.program_id(2)
    start = pl.multiple_of(jnp.maximum(i - 1, 0) * blk, blk)
    tile = jnp.minimum(i, 1)
    q = q_ref[0]
    k = k_ref[0, pl.ds(start, 2 * blk), :]
    v = v_ref[0, pl.ds(start, 2 * blk), :]
    n_part = BRANCH_WIDTH // kvw
    outs = [jnp.zeros((blk, kvw), F32) for _ in range(n_part)]
    lses = []
    for p in range(N_HEADS):
        part, off = divmod(p * HEAD_DIM, kvw)
        lanes = _head_lanes(kvw, off, HEAD_DIM)
        qp = q[:, part * kvw:(part + 1) * kvw]
        s = _nt_dot(jnp.where(lanes, qp, jnp.zeros_like(qp)), k) + bias_ref[tile, p]
        m = jnp.max(s, axis=-1, keepdims=True)
        if has_sinks:
            m = jnp.maximum(m, sinks_ref[p])
        e = jnp.exp(s - m)
        den = jnp.sum(e, axis=-1, keepdims=True)
        if has_sinks:
            den = den + jnp.exp(sinks_ref[p] - m)
        vm = jnp.where(lanes, v, jnp.zeros_like(v))
        outs[part] = outs[part] + _dot(e.astype(BF16), vm) * (1.0 / den)
        if need_lse:
            lses.append(m + jnp.log(den))
    o = outs[0] if n_part == 1 else jnp.concatenate(outs, axis=-1)
    o_ref[0] = o.astype(o_ref.dtype)
    if need_lse:
        lse_ref[0] = _per_head_full(lses)


def _band(h3, bias, sinks, *, dil, q_col, k_col, v_col, kvw, out_dtype):
    bsz, seq, width = h3.shape
    sub = seq // dil
    blk = BAND_BLOCK
    hv = h3.reshape(bsz, sub, dil * width)
    qb, kb = width // BRANCH_WIDTH, width // kvw
    has_sinks = sinks is not None
    need_lse = not has_sinks
    in_specs = [
        pl.BlockSpec((1, blk, BRANCH_WIDTH), lambda b, r, i: (b, i, r * qb + q_col)),
        pl.BlockSpec((1, sub, kvw), lambda b, r, i: (b, 0, r * kb + k_col)),
        pl.BlockSpec((1, sub, kvw), lambda b, r, i: (b, 0, r * kb + v_col)),
        pl.BlockSpec((2, N_HEADS, blk, 2 * blk), lambda b, r, i: (0, 0, 0, 0)),
    ]
    args = [hv, hv, hv, bias]
    if has_sinks:
        in_specs.insert(0, pl.BlockSpec(memory_space=pltpu.SMEM))
        args.insert(0, sinks)
    o_spec = pl.BlockSpec((1, blk, BRANCH_WIDTH), lambda b, r, i: (b, i, r))
    o_shape = jax.ShapeDtypeStruct((bsz, sub, dil * BRANCH_WIDTH), out_dtype)
    out_specs, out_shape = [o_spec], [o_shape]
    if need_lse:
        out_specs.append(o_spec)
        out_shape.append(jax.ShapeDtypeStruct((bsz, sub, dil * BRANCH_WIDTH), F32))
    res = pl.pallas_call(
        functools.partial(_band_kernel, kvw=kvw, has_sinks=has_sinks, need_lse=need_lse),
        grid=(bsz, dil, sub // blk),
        in_specs=in_specs,
        out_specs=out_specs,
        out_shape=out_shape,
        compiler_params=_cparams("parallel", "parallel", "arbitrary"),
        name=f"band_d{dil}" + ("_sink" if has_sinks else ""),
    )(*args)
    return [r.reshape(bsz * seq, BRANCH_WIDTH) for r in res]


def _merge_kernel(x_ref, ya_ref, yb_ref, yc_ref, o1_ref, o2_ref, o3_ref, l1_ref, l2_ref, l3_ref,
                  wg_ref, wb_ref, wo_ref, g_ref, b_ref, out_ref, outb_ref):
    x = x_ref[...]
    xb = x.astype(BF16)
    l1, l2, l3 = l1_ref[...], l2_ref[...], l3_ref[...]
    mx = jnp.maximum(jnp.maximum(l1, l2), l3)
    e1, e2, e3 = jnp.exp(l1 - mx), jnp.exp(l2 - mx), jnp.exp(l3 - mx)
    yd = (e1 * o1_ref[...] + e2 * o2_ref[...] + e3 * o3_ref[...]) / (e1 + e2 + e3)
    ys = [ya_ref[...], yb_ref[...], yc_ref[...], yd.astype(BF16)]
    merged = jnp.zeros((x.shape[0], D_MODEL), F32)
    for i in range(4):
        gate = _dot(xb, wg_ref[:, i * D_MODEL:(i + 1) * D_MODEL])
        merged = merged + _sigmoid(gate) * _dot(ys[i], wb_ref[i])
    mix = _dot(merged.astype(BF16), wo_ref[...])
    out = _layer_norm(ALPHA * x + mix, g_ref[...], b_ref[...])
    out_ref[...] = out
    outb_ref[...] = out.astype(BF16)


def _merge(x2d, ya, yb, yc, od, ld, w_gates, w_branch, w_out, ln_g, ln_b):
    n = x2d.shape[0]
    tm = ROW_TILE
    row = lambda w: pl.BlockSpec((tm, w), lambda i: (i, 0))
    full = lambda *shape: pl.BlockSpec(shape, lambda i: (0,) * len(shape))
    return pl.pallas_call(
        _merge_kernel,
        grid=(n // tm,),
        in_specs=[row(D_MODEL)] + [row(BRANCH_WIDTH)] * 9 + [
            full(D_MODEL, 4 * D_MODEL), full(4, BRANCH_WIDTH, D_MODEL), full(D_MODEL, D_MODEL),
            full(1, D_MODEL), full(1, D_MODEL)],
        out_specs=[row(D_MODEL), row(D_MODEL)],
        out_shape=[jax.ShapeDtypeStruct((n, D_MODEL), F32), jax.ShapeDtypeStruct((n, D_MODEL), BF16)],
        compiler_params=_cparams("parallel"),
        name="merge",
    )(x2d, ya, yb, yc, *od, *ld, w_gates, w_branch, w_out, ln_g, ln_b)


def _row_iota(rows, cols):
    return lax.broadcasted_iota(jnp.int32, (rows, cols), 0).astype(F32)


def _first_max(vals, iota, size):
    m = jnp.max(vals, axis=0, keepdims=True)
    idx = jnp.min(jnp.where(vals == m, iota, float(size)), axis=0, keepdims=True)
    return m, idx


def _router_kernel(x_ref, wr_ref, rb_ref, idx_ref, w_ref):
    tm = x_ref.shape[0]
    logits = _nt_dot(wr_ref[...], x_ref[...])
    scores = _sigmoid(logits)
    choice = scores + rb_ref[...]
    g_iota = _row_iota(GROUP_SIZE, tm)
    group_rows = []
    for g in range(N_GROUPS):
        c = choice[g * GROUP_SIZE:(g + 1) * GROUP_SIZE, :]
        m1, i1 = _first_max(c, g_iota, GROUP_SIZE)
        m2 = jnp.max(jnp.where(g_iota == i1, NEG_INF, c), axis=0, keepdims=True)
        group_rows.append(m1 + m2)
    gs = jnp.concatenate(group_rows, axis=0)
    n_iota = _row_iota(N_GROUPS, tm)
    picked = jnp.zeros((N_GROUPS, tm), F32)
    for _ in range(TOPK_GROUPS):
        _, gi = _first_max(gs, n_iota, N_GROUPS)
        hit = n_iota == gi
        picked = jnp.where(hit, 1.0, picked)
        gs = jnp.where(hit, NEG_INF, gs)
    masked = jnp.concatenate(
        [jnp.where(picked[g:g + 1, :] > 0.0, choice[g * GROUP_SIZE:(g + 1) * GROUP_SIZE, :], NEG_INF)
         for g in range(N_GROUPS)], axis=0)
    e_iota = _row_iota(N_EXPERTS, tm)
    idxs, sels = [], []
    for _ in range(TOP_K):
        _, ei = _first_max(masked, e_iota, N_EXPERTS)
        hit = e_iota == ei
        idxs.append(ei)
        sels.append(jnp.sum(jnp.where(hit, scores, 0.0), axis=0, keepdims=True))
        masked = jnp.where(hit, NEG_INF, masked)
    sel = jnp.concatenate(sels, axis=0)
    idx_ref[...] = jnp.concatenate(idxs, axis=0).astype(jnp.int32)
    w_ref[...] = sel / jnp.sum(sel, axis=0, keepdims=True) * ROUTED_SCALE


def _router(x1b, w_router_t, router_bias):
    n = x1b.shape[0]
    tm = ROW_TILE
    return pl.pallas_call(
        _router_kernel,
        grid=(n // tm,),
        in_specs=[
            pl.BlockSpec((tm, D_MODEL), lambda i: (i, 0)),
            pl.BlockSpec((N_EXPERTS, D_MODEL), lambda i: (0, 0)),
            pl.BlockSpec((N_EXPERTS, 1), lambda i: (0, 0)),
        ],
        out_specs=[pl.BlockSpec((TOP_K, tm), lambda i: (0, i))] * 2,
        out_shape=[jax.ShapeDtypeStruct((TOP_K, n), jnp.int32), jax.ShapeDtypeStruct((TOP_K, n), F32)],
        compiler_params=_cparams("parallel"),
        name="router",
    )(x1b, w_router_t, router_bias)


def _expert_kernel(blk_e_ref, n_used_ref, xs_ref, rw_ref, wg_ref, wu_ref, wd_ref, y_ref):
    i = pl.program_id(0)

    @pl.when(i < n_used_ref[0])
    def _():
        xs = xs_ref[...]
        gate = _dot(xs, wg_ref[0])
        up = _dot(xs, wu_ref[0])
        hidden = (gate * _sigmoid(gate) * up).astype(BF16)
        y_ref[...] = _dot(hidden, wd_ref[0]) * rw_ref[...]

    @pl.when(i >= n_used_ref[0])
    def _():
        y_ref[...] = jnp.zeros(y_ref.shape, y_ref.dtype)


def _experts(blk_e, n_used, xs, row_w, w_gate_e, w_up_e, w_down_e):
    rows = xs.shape[0]
    tb = EXPERT_TILE
    grid_spec = pltpu.PrefetchScalarGridSpec(
        num_scalar_prefetch=2,
        grid=(rows // tb,),
        in_specs=[
            pl.BlockSpec((tb, D_MODEL), lambda i, be, nu: (i, 0)),
            pl.BlockSpec((tb, 1), lambda i, be, nu: (i, 0)),
            pl.BlockSpec((1, D_MODEL, EXPERT_DIM), lambda i, be, nu: (be[i], 0, 0)),
            pl.BlockSpec((1, D_MODEL, EXPERT_DIM), lambda i, be, nu: (be[i], 0, 0)),
            pl.BlockSpec((1, EXPERT_DIM, D_MODEL), lambda i, be, nu: (be[i], 0, 0)),
        ],
        out_specs=pl.BlockSpec((tb, D_MODEL), lambda i, be, nu: (i, 0)),
    )
    return pl.pallas_call(
        _expert_kernel,
        grid_spec=grid_spec,
        out_shape=jax.ShapeDtypeStruct((rows, D_MODEL), F32),
        compiler_params=_cparams("arbitrary"),
        name="experts",
    )(blk_e, n_used, xs, row_w, w_gate_e, w_up_e, w_down_e)


def _ffn_kernel(x_ref, xb_ref, r_ref, wg_ref, wu_ref, wd_ref, g_ref, b_ref, out_ref):
    xb = xb_ref[...]
    gate = _dot(xb, wg_ref[...])
    up = _dot(xb, wu_ref[...])
    shared = _dot((gate * _sigmoid(gate) * up).astype(BF16), wd_ref[...])
    out_ref[...] = _layer_norm(ALPHA * x_ref[...] + (shared + r_ref[...]), g_ref[...], b_ref[...])


def _ffn(x1, x1b, routed, w_gate_s, w_up_s, w_down_s, ln_g, ln_b):
    n = x1.shape[0]
    tm = ROW_TILE
    row = pl.BlockSpec((tm, D_MODEL), lambda i: (i, 0))
    full = lambda *shape: pl.BlockSpec(shape, lambda i: (0,) * len(shape))
    return pl.pallas_call(
        _ffn_kernel,
        grid=(n // tm,),
        in_specs=[row, row, row, full(D_MODEL, EXPERT_DIM), full(D_MODEL, EXPERT_DIM),
                  full(EXPERT_DIM, D_MODEL), full(1, D_MODEL), full(1, D_MODEL)],
        out_specs=row,
        out_shape=jax.ShapeDtypeStruct((n, D_MODEL), F32),
        compiler_params=_cparams("parallel"),
        name="ffn",
    )(x1, x1b, routed, w_gate_s, w_up_s, w_down_s, ln_g, ln_b)


def _t5_bucket(dist):
    n = jnp.maximum(dist, 0)
    exact = T5_BUCKETS // 2
    nf = jnp.maximum(n, 1).astype(F32)
    large = exact + (jnp.log(nf / exact) / math.log(T5_MAX_DIST / exact) * (T5_BUCKETS - exact)).astype(jnp.int32)
    large = jnp.minimum(large, T5_BUCKETS - 1)
    return jnp.where(n < exact, n, large)


def _bias_by_distance(table, max_dist):
    return table[_t5_bucket(jnp.arange(max_dist + 1, dtype=jnp.int32))].T


def _toeplitz_tiles(bias_vec, dist, valid):
    safe = np.where(valid, dist, 0).astype(np.int32)
    tiles = bias_vec[:, safe]
    tiles = jnp.where(valid[None], tiles, NEG_INF)
    return jnp.moveaxis(tiles, 0, -3)


def _band_bias(table, max_dist, dil):
    blk = BAND_BLOCK
    r = np.arange(blk)[:, None]
    j = np.arange(2 * blk)[None, :]
    dist = np.stack([r - j, blk + r - j])
    valid = (dist >= 0) & (dist <= max_dist)
    vec = _bias_by_distance(table, max_dist * dil)[:, ::dil]
    return _toeplitz_tiles(vec, dist, valid)


def _diff_bias(table, seq):
    t, tk = FLASH_TILE, KV_TILE
    saturation = int(math.ceil((T5_BUCKETS // 2) * (T5_MAX_DIST / (T5_BUCKETS // 2)) ** ((T5_BUCKETS // 2 - 1) / (T5_BUCKETS // 2)) * 1.05))
    n_bias = min(seq // tk, -(-(saturation - 1) // tk) + 1 + KV_PER_Q)
    key = np.arange(tk)[:, None]
    query = np.arange(t)[None, :]
    dist = np.stack([(i - (KV_PER_Q - 1)) * tk + query - key for i in range(n_bias)])
    valid = dist >= 0
    vec = _bias_by_distance(table, min(seq, n_bias * tk + t))
    return _toeplitz_tiles(vec, dist, valid)


SWA_HEAD_ORDER = (0, 2, 1, 3)


def _pack_qkv_weights(w_in):
    offs = np.cumsum([0, 256, 256, 256, 4, 256, 128, 128, 256, 256, 256, 256, 256, 256])
    (qa, ka, va, fa, qb, kb, vb, qc, kc, vc, qd, kd, vd) = [int(o) for o in offs[:13]]
    gates = int(offs[13])
    col = lambda o, w: w_in[:, o:o + w]
    scale = HEAD_DIM ** -0.5
    qb_w = col(qb, 256).reshape(D_MODEL, N_HEADS, HEAD_DIM)[:, np.asarray(SWA_HEAD_ORDER)].reshape(D_MODEL, 256)
    w_rows = jnp.concatenate([
        col(ka, 256), qb_w * scale, col(kb, 128), col(vb, 128), col(kc, 256),
        col(qd, 256) * scale, col(kd, 256), col(vd, 256)], axis=1).astype(BF16)
    w_t = jnp.concatenate([col(qa, 256) * scale, col(va, 256), col(qc, 256), col(vc, 256)], axis=1).T.astype(BF16)
    w_f = jnp.pad(col(fa, 4), ((0, 0), (0, 124))).astype(BF16)
    w_gates = col(gates, 4 * D_MODEL).astype(BF16)
    return w_rows, w_t, w_f, w_gates


def _dispatch(idx_t, wts_t, n_tok):
    tb = EXPERT_TILE
    nk = n_tok * TOP_K
    nblk = nk // tb + N_EXPERTS
    rows = nblk * tb
    flat_e = idx_t.T.reshape(-1)
    flat_w = wts_t.T.reshape(-1)
    order = jnp.argsort(flat_e)
    se = flat_e[order]
    counts = jnp.bincount(flat_e, length=N_EXPERTS)
    starts = jnp.cumsum(counts) - counts
    padded = (counts + tb - 1) // tb * tb
    pends = jnp.cumsum(padded)
    pstarts = pends - padded
    dest = (pstarts[se] + jnp.arange(nk) - starts[se]).astype(jnp.int32)
    row_tok = jnp.full((rows,), n_tok, jnp.int32).at[dest].set((order // TOP_K).astype(jnp.int32))
    row_w = jnp.zeros((rows,), F32).at[dest].set(flat_w[order])
    blk_e = jnp.minimum(jnp.searchsorted(pends, jnp.arange(nblk) * tb, side="right"), N_EXPERTS - 1).astype(jnp.int32)
    n_used = (pends[-1] // tb).astype(jnp.int32).reshape(1)
    slot = jnp.zeros((nk,), jnp.int32).at[order].set(dest)
    return row_tok, row_w, blk_e, n_used, slot


def kernel(x, t5_table, w_in, b_forget, attn_sinks, lam_q1, lam_k1, lam_q2, lam_k2, diff_norm_g, w_branch, w_out, ln1_g, ln1_b, w_router, router_bias, w_gate_e, w_up_e, w_down_e, w_gate_s, w_up_s, w_down_s, ln2_g, ln2_b):
    bsz, seq, d = x.shape
    n_tok = bsz * seq
    t = FLASH_TILE
    table = t5_table.astype(F32)
    swa_bias = _band_bias(table[:, np.asarray(SWA_HEAD_ORDER)], SWA_WINDOW - 1, 1)
    diff_bias = _diff_bias(table[:, N_HEADS:2 * N_HEADS], seq) * LOG2_E
    dil_bias = [_band_bias(table[:, 2 * N_HEADS:], window // dil, dil) for window, dil in DIL_PATTERNS]

    x2d = x.reshape(n_tok, d)
    for layer in range(DEPTH):
        lam_init = 0.8 - 0.6 * math.exp(-0.3 * layer)
        w_rows, w_t, w_f, w_gates = _pack_qkv_weights(w_in[layer])
        h3, qt_a, vt_a, qt_c, vt_c, f_logit = _proj(x2d.reshape(bsz, seq, d), w_rows, w_t, w_f)

        log_f = jax.nn.log_sigmoid(f_logit[:, :, :N_HEADS] + b_forget[layer].astype(F32))
        c = jnp.cumsum(log_f, axis=1)
        cq = jnp.pad(c.transpose(0, 2, 1), ((0, 0), (0, 8 - N_HEADS), (0, 0)))
        ya = _fox(qt_a, h3, vt_a, cq, c).reshape(n_tok, BRANCH_WIDTH)

        sinks = attn_sinks[layer].astype(F32)[jnp.asarray(SWA_HEAD_ORDER)]
        (yb,) = _band(h3, swa_bias, sinks, dil=1, q_col=1, k_col=4, v_col=5, kvw=128, out_dtype=BF16)

        lam = (jnp.exp(jnp.sum(lam_q1[layer].astype(F32) * lam_k1[layer].astype(F32)))
               - jnp.exp(jnp.sum(lam_q2[layer].astype(F32) * lam_k2[layer].astype(F32))) + lam_init)
        g_col = (jnp.tile(diff_norm_g[layer].astype(F32), N_HEADS) * (1.0 - lam_init)).reshape(BRANCH_WIDTH, 1)
        yc = _diff(qt_c, h3, vt_c, lam.reshape(1), diff_bias, g_col).reshape(n_tok, BRANCH_WIDTH)

        od, ld = [], []
        for (window, dil), bias in zip(DIL_PATTERNS, dil_bias):
            o, lse = _band(h3, bias, None, dil=dil, q_col=4, k_col=5, v_col=6, kvw=256, out_dtype=F32)
            od.append(o)
            ld.append(lse)

        wb = w_branch[layer]
        wb = jnp.stack([wb[0], wb[1].reshape(N_HEADS, HEAD_DIM, d)[np.asarray(SWA_HEAD_ORDER)].reshape(BRANCH_WIDTH, d),
                        wb[2], wb[3]]).astype(BF16)
        x1, x1b = _merge(x2d, ya, yb, yc, od, ld, w_gates, wb, w_out[layer].astype(BF16),
                         ln1_g[layer].reshape(1, d), ln1_b[layer].reshape(1, d))

        idx_t, wts_t = _router(x1b, w_router[layer].T.astype(BF16), router_bias[layer].astype(F32).reshape(N_EXPERTS, 1))
        row_tok, row_w, blk_e, n_used, slot = _dispatch(idx_t, wts_t, n_tok)
        xs = jnp.concatenate([x1b, jnp.zeros((1, d), BF16)], axis=0)[row_tok]
        y = _experts(blk_e, n_used, xs, row_w[:, None], w_gate_e[layer].astype(BF16),
                     w_up_e[layer].astype(BF16), w_down_e[layer].astype(BF16))
        routed = y[slot].reshape(n_tok, TOP_K, d).sum(axis=1)

        x2d = _ffn(x1, x1b, routed, w_gate_s[layer].astype(BF16), w_up_s[layer].astype(BF16),
                   w_down_s[layer].astype(BF16), ln2_g[layer].reshape(1, d), ln2_b[layer].reshape(1, d))
    return x2d.reshape(bsz, seq, d)
```

```python
import functools
import math

import numpy as np
import jax
import jax.numpy as jnp
from jax import lax
from jax.experimental import pallas as pl
from jax.experimental.pallas import tpu as pltpu

F32 = jnp.float32
BF16 = jnp.bfloat16
NEG_INF = float("-inf")

D_MODEL = 1024
DEPTH = 2
HEAD_DIM = 64
N_HEADS = 4
SWA_KV_HEADS = 2
SWA_WINDOW = 128
DIFF_QK_DIM = 32
DIL_PATTERNS = ((128, 1), (512, 4), (2048, 16))
BRANCH_WIDTH = N_HEADS * HEAD_DIM
BAND_BLOCK = 128
T5_BUCKETS = 32
T5_MAX_DIST = 2048
N_EXPERTS = 64
TOP_K = 8
N_GROUPS = 8
TOPK_GROUPS = 4
GROUP_SIZE = N_EXPERTS // N_GROUPS
EXPERT_DIM = 256
ROUTED_SCALE = 2.5
ALPHA = (2 * DEPTH) ** 0.25
LN_EPS = 1e-5
LOG2_E = math.log2(math.e)

ROWS_WIDTH = 7 * BRANCH_WIDTH
FLASH_TILE = 256
KV_TILE = 128
KV_PER_Q = FLASH_TILE // KV_TILE
ROW_TILE = 512
EXPERT_TILE = 256
COMBINE_TILE = 256
VMEM_LIMIT = 56 * 1024 * 1024


def _cparams(*sem):
    return pltpu.CompilerParams(dimension_semantics=sem, vmem_limit_bytes=VMEM_LIMIT)


def _nt_dot(a, b):
    return lax.dot_general(a, b, (((1,), (1,)), ((), ())), preferred_element_type=F32)


def _dot(a, b):
    return jnp.dot(a, b, preferred_element_type=F32)


def _sigmoid(x):
    return 1.0 / (1.0 + jnp.exp(-x))


def _layer_norm(z, g, b):
    mu = jnp.mean(z, axis=-1, keepdims=True)
    zc = z - mu
    var = jnp.mean(zc * zc, axis=-1, keepdims=True)
    return zc * lax.rsqrt(var + LN_EPS) * g + b


def _head_lanes(width, lo, size):
    lane = lax.broadcasted_iota(jnp.int32, (1, width), 1)
    return (lane >= lo) & (lane < lo + size)


def _proj_kernel(x_ref, w_ref, wt_ref, wf_ref, h_ref, qta_ref, vta_ref, qtc_ref, vtc_ref, f_ref):
    t = KV_TILE
    xb = x_ref[0].astype(BF16)
    h_ref[0] = _dot(xb, w_ref[...]).astype(BF16)
    f_ref[0] = _dot(xb, wf_ref[...])
    tt = _nt_dot(wt_ref[...], xb)
    w = BRANCH_WIDTH
    qta_ref[0] = tt[0:w].astype(BF16)
    qtc_ref[0] = tt[2 * w:3 * w].astype(BF16)
    for j in range(xb.shape[0] // t):
        vta_ref[0, j] = tt[w:2 * w, j * t:(j + 1) * t].astype(BF16)
        vtc_ref[0, j] = tt[3 * w:4 * w, j * t:(j + 1) * t].astype(BF16)


def _proj(x3, w_rows, w_t, w_f):
    bsz, seq, d = x3.shape
    tm, t, w = ROW_TILE, KV_TILE, BRANCH_WIDTH
    qt_spec = pl.BlockSpec((1, w, tm), lambda b, i: (b, 0, i))
    vt_spec = pl.BlockSpec((1, tm // t, w, t), lambda b, i: (b, i, 0, 0))
    qt_shape = jax.ShapeDtypeStruct((bsz, w, seq), BF16)
    vt_shape = jax.ShapeDtypeStruct((bsz, seq // t, w, t), BF16)
    return pl.pallas_call(
        _proj_kernel,
        grid=(bsz, seq // tm),
        in_specs=[
            pl.BlockSpec((1, tm, d), lambda b, i: (b, i, 0)),
            pl.BlockSpec((d, ROWS_WIDTH), lambda b, i: (0, 0)),
            pl.BlockSpec((4 * w, d), lambda b, i: (0, 0)),
            pl.BlockSpec((d, 128), lambda b, i: (0, 0)),
        ],
        out_specs=[
            pl.BlockSpec((1, tm, ROWS_WIDTH), lambda b, i: (b, i, 0)),
            qt_spec, vt_spec, qt_spec, vt_spec,
            pl.BlockSpec((1, tm, 128), lambda b, i: (b, i, 0)),
        ],
        out_shape=[
            jax.ShapeDtypeStruct((bsz, seq, ROWS_WIDTH), BF16),
            qt_shape, vt_shape, qt_shape, vt_shape,
            jax.ShapeDtypeStruct((bsz, seq, 128), F32),
        ],
        compiler_params=_cparams("parallel", "parallel"),
        name="proj",
    )(x3, w_rows, w_t, w_f)


def _online_update(s, m_ref, l_ref, idx, exp=jnp.exp):
    row = slice(idx, idx + 1)
    m_prev = m_ref[row, :]
    m_new = jnp.maximum(m_prev, jnp.max(s, axis=0, keepdims=True))
    a = exp(m_prev - m_new)
    p = exp(s - m_new)
    l_ref[row, :] = a * l_ref[row, :] + jnp.sum(p, axis=0, keepdims=True)
    m_ref[row, :] = m_new
    return p, a


def _band_rows(x, lo, size):
    rows, cols = x.shape
    parts = []
    if lo:
        parts.append(jnp.zeros((lo, cols), x.dtype))
    parts.append(x[lo:lo + size])
    if lo + size < rows:
        parts.append(jnp.zeros((rows - lo - size, cols), x.dtype))
    return jnp.concatenate(parts, axis=0)


def _per_head_full(cols, width=BRANCH_WIDTH):
    full = cols[-1]
    for h in range(len(cols) - 2, -1, -1):
        full = jnp.where(_head_lanes(width, h * HEAD_DIM, HEAD_DIM), cols[h], full)
    return full


def _flash_init(m_ref, l_ref, *acc_refs):
    m_ref[...] = jnp.full(m_ref.shape, NEG_INF, F32)
    l_ref[...] = jnp.zeros(l_ref.shape, F32)
    for acc_ref in acc_refs:
        acc_ref[...] = jnp.zeros(acc_ref.shape, F32)


def _fox_kernel(qt_ref, k_ref, vt_ref, cq_ref, ck_ref, o_ref, qm_ref, acc_ref, m_ref, l_ref):
    t, tk = FLASH_TILE, KV_TILE
    qi = pl.program_id(1)
    qt = qt_ref[0]
    for h in range(N_HEADS):
        qm_ref[h] = _band_rows(qt[128 * (h // 2):128 * (h // 2 + 1)], HEAD_DIM * (h % 2), HEAD_DIM)
    _flash_init(m_ref, l_ref, acc_ref)
    cq = cq_ref[0]

    def step(ki, diagonal):
        start = pl.multiple_of(ki * tk, tk)
        k = k_ref[0, pl.ds(start, tk), :]
        ck = ck_ref[0, pl.ds(start, tk), :]
        qk = lambda h: _dot(k[:, 128 * (h // 2):128 * (h // 2 + 1)], qm_ref[h])
        ahead = 2
        pending = [qk(h) for h in range(ahead)]
        for h in range(N_HEADS):
            rows = slice(h * HEAD_DIM, (h + 1) * HEAD_DIM)
            s = pending.pop(0)
            if h + ahead < N_HEADS:
                pending.append(qk(h + ahead))
            s = s + (cq[h:h + 1, :] - ck[:, h:h + 1])
            if diagonal is not None:
                key = lax.broadcasted_iota(jnp.int32, (tk, t), 0) + diagonal * tk
                query = lax.broadcasted_iota(jnp.int32, (tk, t), 1)
                s = jnp.where(key <= query, s, NEG_INF)
            p, a = _online_update(s, m_ref, l_ref, h)
            acc_ref[rows, :] = a * acc_ref[rows, :] + _dot(vt_ref[0, ki, rows, :], p.astype(BF16))

    def body(ki, carry):
        step(ki, None)
        return carry

    lax.fori_loop(0, qi * KV_PER_Q, body, 0)
    for j in range(KV_PER_Q):
        step(qi * KV_PER_Q + j, j)
    yt = jnp.concatenate([acc_ref[h * HEAD_DIM:(h + 1) * HEAD_DIM, :] * (1.0 / l_ref[h:h + 1, :])
                          for h in range(N_HEADS)], axis=0)
    o_ref[0] = yt.T.astype(o_ref.dtype)


def _flash_specs(bsz, seq, k_col):
    t, w = FLASH_TILE, BRANCH_WIDTH
    in_specs = [
        pl.BlockSpec((1, w, t), lambda b, i: (b, 0, i)),
        pl.BlockSpec((1, seq, w), lambda b, i: (b, 0, k_col)),
        pl.BlockSpec((1, seq // KV_TILE, w, KV_TILE), lambda b, i: (b, 0, 0, 0)),
    ]
    out_spec = pl.BlockSpec((1, t, w), lambda b, i: (b, i, 0))
    out_shape = jax.ShapeDtypeStruct((bsz, seq, w), BF16)
    return in_specs, out_spec, out_shape


def _fox(qt, h3, vt, cq, ck):
    bsz, seq, _ = h3.shape
    t, w = FLASH_TILE, BRANCH_WIDTH
    in_specs, out_spec, out_shape = _flash_specs(bsz, seq, 0)
    return pl.pallas_call(
        _fox_kernel,
        grid=(bsz, seq // t),
        in_specs=in_specs + [
            pl.BlockSpec((1, 8, t), lambda b, i: (b, 0, i)),
            pl.BlockSpec((1, seq, N_HEADS), lambda b, i: (b, 0, 0)),
        ],
        out_specs=out_spec,
        out_shape=out_shape,
        scratch_shapes=[
            pltpu.VMEM((N_HEADS, 128, t), BF16),
            pltpu.VMEM((w, t), F32),
            pltpu.VMEM((8, t), F32),
            pltpu.VMEM((8, t), F32),
        ],
        compiler_params=_cparams("parallel", "arbitrary"),
        name="fox",
    )(qt, h3, vt, cq, ck)


def _diff_kernel(lam_ref, qt_ref, k_ref, vt_ref, bias_ref, g_ref, o_ref,
                 qm_ref, acc1_ref, acc2_ref, m_ref, l_ref, *, n_bias):
    t, tk = FLASH_TILE, KV_TILE
    qi = pl.program_id(1)
    qt = qt_ref[0]
    for h in range(N_HEADS):
        for j in range(2):
            qm_ref[2 * h + j] = _band_rows(qt[128 * (h // 2):128 * (h // 2 + 1)],
                                           HEAD_DIM * (h % 2) + DIFF_QK_DIM * j, DIFF_QK_DIM)
    _flash_init(m_ref, l_ref, acc1_ref, acc2_ref)
    accs = (acc1_ref, acc2_ref)
    scale = DIFF_QK_DIM ** -0.5 * LOG2_E

    def body(ki, carry):
        bias_idx = jnp.minimum(qi * KV_PER_Q + (KV_PER_Q - 1) - ki, n_bias - 1)
        start = pl.multiple_of(ki * tk, tk)
        k = k_ref[0, pl.ds(start, tk), :]
        qk = lambda i: _dot(k[:, 128 * (i // 4):128 * (i // 4 + 1)], qm_ref[i])
        ahead = 3
        pending = [qk(i) for i in range(ahead)]
        for i in range(2 * N_HEADS):
            h, j = divmod(i, 2)
            rows = slice(h * HEAD_DIM, (h + 1) * HEAD_DIM)
            s = pending.pop(0)
            if i + ahead < 2 * N_HEADS:
                pending.append(qk(i + ahead))
            s = s * scale + bias_ref[bias_idx, h]
            p, a = _online_update(s, m_ref, l_ref, i, exp=jnp.exp2)
            accs[j][rows, :] = a * accs[j][rows, :] + _dot(vt_ref[0, ki, rows, :], p.astype(BF16))
        return carry

    lax.fori_loop(0, (qi + 1) * KV_PER_Q, body, 0)
    outs = []
    for h in range(N_HEADS):
        rows = slice(h * HEAD_DIM, (h + 1) * HEAD_DIM)
        y = (acc1_ref[rows, :] * (1.0 / l_ref[2 * h:2 * h + 1, :])
             - lam_ref[0] * (acc2_ref[rows, :] * (1.0 / l_ref[2 * h + 1:2 * h + 2, :])))
        ms = jnp.mean(y * y, axis=0, keepdims=True)
        outs.append(y * lax.rsqrt(ms + LN_EPS) * g_ref[rows, :])
    o_ref[0] = jnp.concatenate(outs, axis=0).T.astype(o_ref.dtype)


def _diff(qt, h3, vt, lam, bias, g_col):
    bsz, seq, _ = h3.shape
    t, w = FLASH_TILE, BRANCH_WIDTH
    n_bias = bias.shape[0]
    in_specs, out_spec, out_shape = _flash_specs(bsz, seq, 3)
    return pl.pallas_call(
        functools.partial(_diff_kernel, n_bias=n_bias),
        grid=(bsz, seq // t),
        in_specs=[pl.BlockSpec(memory_space=pltpu.SMEM)] + in_specs + [
            pl.BlockSpec((n_bias, N_HEADS, KV_TILE, t), lambda b, i: (0, 0, 0, 0)),
            pl.BlockSpec((w, 1), lambda b, i: (0, 0)),
        ],
        out_specs=out_spec,
        out_shape=out_shape,
        scratch_shapes=[
            pltpu.VMEM((2 * N_HEADS, 128, t), BF16),
            pltpu.VMEM((w, t), F32),
            pltpu.VMEM((w, t), F32),
            pltpu.VMEM((8, t), F32),
            pltpu.VMEM((8, t), F32),
        ],
        compiler_params=_cparams("parallel", "arbitrary"),
        name="diff",
    )(lam, qt, h3, vt, bias, g_col)


def _band_kernel(*refs, kvw, has_sinks, need_lse):
    refs = list(refs)
    sinks_ref = refs.pop(0) if has_sinks else None
    q_ref, k_ref, v_ref, bias_ref, o_ref = refs[:5]
    lse_ref = refs[5] if need_lse else None
    blk = BAND_BLOCK
    i = pl.program_id(2)
    start = pl.multiple_of(jnp.maximum(i - 1, 0) * blk, blk)
    tile = jnp.minimum(i, 1)
    q = q_ref[0]
    k = k_ref[0, pl.ds(start, 2 * blk), :]
    v = v_ref[0, pl.ds(start, 2 * blk), :]
    n_part = BRANCH_WIDTH // kvw
    outs = [jnp.zeros((blk, kvw), F32) for _ in range(n_part)]
    lses = []
    for p in range(N_HEADS):
        part, off = divmod(p * HEAD_DIM, kvw)
        lanes = _head_lanes(kvw, off, HEAD_DIM)
        qp = q[:, part * kvw:(part + 1) * kvw]
        s = _nt_dot(jnp.where(lanes, qp, jnp.zeros_like(qp)), k) + bias_ref[tile, p]
        m = jnp.max(s, axis=-1, keepdims=True)
        if has_sinks:
            m = jnp.maximum(m, sinks_ref[p])
        e = jnp.exp(s - m)
        den = jnp.sum(e, axis=-1, keepdims=True)
        if has_sinks:
            den = den + jnp.exp(sinks_ref[p] - m)
        vm = jnp.where(lanes, v, jnp.zeros_like(v))
        outs[part] = outs[part] + _dot(e.astype(BF16), vm) * (1.0 / den)
        if need_lse:
            lses.append(m + jnp.log(den))
    o = outs[0] if n_part == 1 else jnp.concatenate(outs, axis=-1)
    o_ref[0] = o.astype(o_ref.dtype)
    if need_lse:
        lse_ref[0] = _per_head_full(lses)


def _band(h3, bias, sinks, *, dil, q_col, k_col, v_col, kvw, out_dtype):
    bsz, seq, width = h3.shape
    sub = seq // dil
    blk = BAND_BLOCK
    hv = h3.reshape(bsz, sub, dil * width)
    qb, kb = width // BRANCH_WIDTH, width // kvw
    has_sinks = sinks is not None
    need_lse = not has_sinks
    in_specs = [
        pl.BlockSpec((1, blk, BRANCH_WIDTH), lambda b, r, i: (b, i, r * qb + q_col)),
        pl.BlockSpec((1, sub, kvw), lambda b, r, i: (b, 0, r * kb + k_col)),
        pl.BlockSpec((1, sub, kvw), lambda b, r, i: (b, 0, r * kb + v_col)),
        pl.BlockSpec((2, N_HEADS, blk, 2 * blk), lambda b, r, i: (0, 0, 0, 0)),
    ]
    args = [hv, hv, hv, bias]
    if has_sinks:
        in_specs.insert(0, pl.BlockSpec(memory_space=pltpu.SMEM))
        args.insert(0, sinks)
    o_spec = pl.BlockSpec((1, blk, BRANCH_WIDTH), lambda b, r, i: (b, i, r))
    o_shape = jax.ShapeDtypeStruct((bsz, sub, dil * BRANCH_WIDTH), out_dtype)
    out_specs, out_shape = [o_spec], [o_shape]
    if need_lse:
        out_specs.append(o_spec)
        out_shape.append(jax.ShapeDtypeStruct((bsz, sub, dil * BRANCH_WIDTH), F32))
    res = pl.pallas_call(
        functools.partial(_band_kernel, kvw=kvw, has_sinks=has_sinks, need_lse=need_lse),
        grid=(bsz, dil, sub // blk),
        in_specs=in_specs,
        out_specs=out_specs,
        out_shape=out_shape,
        compiler_params=_cparams("parallel", "parallel", "arbitrary"),
        name=f"band_d{dil}" + ("_sink" if has_sinks else ""),
    )(*args)
    return [r.reshape(bsz * seq, BRANCH_WIDTH) for r in res]


def _merge_kernel(x_ref, ya_ref, yb_ref, yc_ref, o1_ref, o2_ref, o3_ref, l1_ref, l2_ref, l3_ref,
                  wg_ref, wb_ref, wo_ref, g_ref, b_ref, out_ref, outb_ref):
    x = x_ref[...]
    xb = x.astype(BF16)
    l1, l2, l3 = l1_ref[...], l2_ref[...], l3_ref[...]
    mx = jnp.maximum(jnp.maximum(l1, l2), l3)
    e1, e2, e3 = jnp.exp(l1 - mx), jnp.exp(l2 - mx), jnp.exp(l3 - mx)
    yd = (e1 * o1_ref[...] + e2 * o2_ref[...] + e3 * o3_ref[...]) / (e1 + e2 + e3)
    ys = [ya_ref[...], yb_ref[...], yc_ref[...], yd.astype(BF16)]
    merged = jnp.zeros((x.shape[0], D_MODEL), F32)
    for i in range(4):
        gate = _dot(xb, wg_ref[:, i * D_MODEL:(i + 1) * D_MODEL])
        merged = merged + _sigmoid(gate) * _dot(ys[i], wb_ref[i])
    mix = _dot(merged.astype(BF16), wo_ref[...])
    out = _layer_norm(ALPHA * x + mix, g_ref[...], b_ref[...])
    out_ref[...] = out
    outb_ref[...] = out.astype(BF16)


def _merge(x2d, ya, yb, yc, od, ld, w_gates, w_branch, w_out, ln_g, ln_b):
    n = x2d.shape[0]
    tm = ROW_TILE
    row = lambda w: pl.BlockSpec((tm, w), lambda i: (i, 0))
    full = lambda *shape: pl.BlockSpec(shape, lambda i: (0,) * len(shape))
    return pl.pallas_call(
        _merge_kernel,
        grid=(n // tm,),
        in_specs=[row(D_MODEL)] + [row(BRANCH_WIDTH)] * 9 + [
            full(D_MODEL, 4 * D_MODEL), full(4, BRANCH_WIDTH, D_MODEL), full(D_MODEL, D_MODEL),
            full(1, D_MODEL), full(1, D_MODEL)],
        out_specs=[row(D_MODEL), row(D_MODEL)],
        out_shape=[jax.ShapeDtypeStruct((n, D_MODEL), F32), jax.ShapeDtypeStruct((n, D_MODEL), BF16)],
        compiler_params=_cparams("parallel"),
        name="merge",
    )(x2d, ya, yb, yc, *od, *ld, w_gates, w_branch, w_out, ln_g, ln_b)


def _row_iota(rows, cols):
    return lax.broadcasted_iota(jnp.int32, (rows, cols), 0).astype(F32)


def _first_max(vals, iota, size):
    m = jnp.max(vals, axis=0, keepdims=True)
    idx = jnp.min(jnp.where(vals == m, iota, float(size)), axis=0, keepdims=True)
    return m, idx


def _router_kernel(x_ref, wr_ref, rb_ref, tri_ref, idx_ref, w_ref, rank_ref, cnt_ref, carry_ref):
    tm = x_ref.shape[0]

    @pl.when(pl.program_id(0) == 0)
    def _():
        carry_ref[...] = jnp.zeros(carry_ref.shape, F32)

    logits = _nt_dot(wr_ref[...], x_ref[...])
    scores = _sigmoid(logits)
    choice = scores + rb_ref[...]
    g_iota = _row_iota(GROUP_SIZE, tm)
    group_rows = []
    for g in range(N_GROUPS):
        c = choice[g * GROUP_SIZE:(g + 1) * GROUP_SIZE, :]
        m1, i1 = _first_max(c, g_iota, GROUP_SIZE)
        m2 = jnp.max(jnp.where(g_iota == i1, NEG_INF, c), axis=0, keepdims=True)
        group_rows.append(m1 + m2)
    gs = jnp.concatenate(group_rows, axis=0)
    n_iota = _row_iota(N_GROUPS, tm)
    picked = jnp.zeros((N_GROUPS, tm), F32)
    for _ in range(TOPK_GROUPS):
        _, gi = _first_max(gs, n_iota, N_GROUPS)
        hit = n_iota == gi
        picked = jnp.where(hit, 1.0, picked)
        gs = jnp.where(hit, NEG_INF, gs)
    masked = jnp.concatenate(
        [jnp.where(picked[g:g + 1, :] > 0.0, choice[g * GROUP_SIZE:(g + 1) * GROUP_SIZE, :], NEG_INF)
         for g in range(N_GROUPS)], axis=0)
    e_iota = _row_iota(N_EXPERTS, tm)
    idxs, sels, hits = [], [], []
    assigned = jnp.zeros((N_EXPERTS, tm), F32)
    for _ in range(TOP_K):
        _, ei = _first_max(masked, e_iota, N_EXPERTS)
        hit = e_iota == ei
        idxs.append(ei)
        hits.append(hit)
        sels.append(jnp.sum(jnp.where(hit, scores, 0.0), axis=0, keepdims=True))
        masked = jnp.where(hit, NEG_INF, masked)
        assigned = jnp.where(hit, 1.0, assigned)
    sel = jnp.concatenate(sels, axis=0)
    idx_ref[...] = jnp.concatenate(idxs, axis=0).astype(jnp.int32)
    w_ref[...] = sel / jnp.sum(sel, axis=0, keepdims=True) * ROUTED_SCALE
    carry = carry_ref[...]
    before = _dot(assigned.astype(BF16), tri_ref[...]) + carry
    rank_ref[...] = jnp.concatenate(
        [jnp.sum(jnp.where(hit, before, 0.0), axis=0, keepdims=True) for hit in hits], axis=0).astype(jnp.int32)
    carry = carry + jnp.sum(assigned, axis=1, keepdims=True)
    carry_ref[...] = carry
    cnt_ref[...] = jnp.broadcast_to(carry, cnt_ref.shape)


def _router(x1b, w_router_t, router_bias):
    n = x1b.shape[0]
    tm = ROW_TILE
    tri = jnp.asarray(np.triu(np.ones((tm, tm), np.float32), k=1), BF16)
    tok = pl.BlockSpec((TOP_K, tm), lambda i: (0, i))
    return pl.pallas_call(
        _router_kernel,
        grid=(n // tm,),
        in_specs=[
            pl.BlockSpec((tm, D_MODEL), lambda i: (i, 0)),
            pl.BlockSpec((N_EXPERTS, D_MODEL), lambda i: (0, 0)),
            pl.BlockSpec((N_EXPERTS, 1), lambda i: (0, 0)),
            pl.BlockSpec((tm, tm), lambda i: (0, 0)),
        ],
        out_specs=[tok, tok, tok, pl.BlockSpec((N_EXPERTS, 128), lambda i: (0, 0))],
        out_shape=[jax.ShapeDtypeStruct((TOP_K, n), jnp.int32), jax.ShapeDtypeStruct((TOP_K, n), F32),
                   jax.ShapeDtypeStruct((TOP_K, n), jnp.int32), jax.ShapeDtypeStruct((N_EXPERTS, 128), F32)],
        scratch_shapes=[pltpu.VMEM((N_EXPERTS, 1), F32)],
        compiler_params=_cparams("arbitrary"),
        name="router",
    )(x1b, w_router_t, router_bias, tri)


def _dest_kernel(pstart_ref, idx_ref, rank_ref, dest_ref):
    idx = idx_ref[...]
    base = jnp.zeros(idx.shape, jnp.int32)
    for e in range(N_EXPERTS):
        base = jnp.where(idx == e, pstart_ref[e], base)
    dest_ref[...] = base + rank_ref[...]


def _dest(idx_t, rank_t, pstarts):
    n = idx_t.shape[1]
    tm = 4 * ROW_TILE
    tok = pl.BlockSpec((TOP_K, tm), lambda i: (0, i))
    return pl.pallas_call(
        _dest_kernel,
        grid=(n // tm,),
        in_specs=[pl.BlockSpec(memory_space=pltpu.SMEM), tok, tok],
        out_specs=tok,
        out_shape=jax.ShapeDtypeStruct((TOP_K, n), jnp.int32),
        compiler_params=_cparams("parallel"),
        name="dest",
    )(pstarts, idx_t, rank_t)


def _dispatch_kernel(dest_ref, x_ref, xs_in_ref, xs_ref, sem):
    del xs_in_ref
    tm = x_ref.shape[0]

    def body(t, carry):
        for k in range(TOP_K):
            pltpu.make_async_copy(x_ref.at[pl.ds(t, 1), :], xs_ref.at[pl.ds(dest_ref[k, t], 1), :], sem).start()
        return carry

    lax.fori_loop(0, tm, body, 0)
    done = xs_ref.at[pl.ds(0, TOP_K * tm), :]
    pltpu.make_async_copy(done, done, sem).wait()


def _dispatch(dest, x1, xs_zero):
    n, d = x1.shape
    tm = ROW_TILE
    return pl.pallas_call(
        _dispatch_kernel,
        grid=(n // tm,),
        in_specs=[
            pl.BlockSpec((TOP_K, tm), lambda i: (0, i), memory_space=pltpu.SMEM),
            pl.BlockSpec((tm, d), lambda i: (i, 0)),
            pl.BlockSpec(memory_space=pl.ANY),
        ],
        out_specs=pl.BlockSpec(memory_space=pl.ANY),
        out_shape=jax.ShapeDtypeStruct(xs_zero.shape, xs_zero.dtype),
        scratch_shapes=[pltpu.SemaphoreType.DMA(())],
        input_output_aliases={2: 0},
        compiler_params=_cparams("arbitrary"),
        name="dispatch",
    )(dest, x1, xs_zero)


def _expert_kernel(blk_e_ref, n_used_ref, xs_ref, wg_ref, wu_ref, wd_ref, y_ref):
    i = pl.program_id(0)

    @pl.when(i < n_used_ref[0])
    def _():
        xs = xs_ref[...].astype(BF16)
        gate = _dot(xs, wg_ref[0])
        up = _dot(xs, wu_ref[0])
        hidden = (gate * _sigmoid(gate) * up).astype(BF16)
        y_ref[...] = _dot(hidden, wd_ref[0])

    @pl.when(i >= n_used_ref[0])
    def _():
        y_ref[...] = jnp.zeros(y_ref.shape, y_ref.dtype)


def _experts(blk_e, n_used, xs, w_gate_e, w_up_e, w_down_e):
    rows = xs.shape[0]
    tb = EXPERT_TILE
    grid_spec = pltpu.PrefetchScalarGridSpec(
        num_scalar_prefetch=2,
        grid=(rows // tb,),
        in_specs=[
            pl.BlockSpec((tb, D_MODEL), lambda i, be, nu: (i, 0)),
            pl.BlockSpec((1, D_MODEL, EXPERT_DIM), lambda i, be, nu: (be[i], 0, 0)),
            pl.BlockSpec((1, D_MODEL, EXPERT_DIM), lambda i, be, nu: (be[i], 0, 0)),
            pl.BlockSpec((1, EXPERT_DIM, D_MODEL), lambda i, be, nu: (be[i], 0, 0)),
        ],
        out_specs=pl.BlockSpec((tb, D_MODEL), lambda i, be, nu: (i, 0)),
    )
    return pl.pallas_call(
        _expert_kernel,
        grid_spec=grid_spec,
        out_shape=jax.ShapeDtypeStruct((rows, D_MODEL), F32),
        compiler_params=_cparams("arbitrary"),
        name="experts",
    )(blk_e, n_used, xs, w_gate_e, w_up_e, w_down_e)


def _ffn_kernel(dest_ref, x_ref, xb_ref, rw_ref, y_ref, wg_ref, wu_ref, wd_ref, g_ref, b_ref, out_ref, ybuf_ref, sem):
    tm = x_ref.shape[0]

    def body(t, carry):
        for k in range(TOP_K):
            pltpu.make_async_copy(y_ref.at[pl.ds(dest_ref[k, t], 1), :], ybuf_ref.at[k, pl.ds(t, 1), :], sem).start()
        return carry

    lax.fori_loop(0, tm, body, 0)
    xb = xb_ref[...]
    gate = _dot(xb, wg_ref[...])
    up = _dot(xb, wu_ref[...])
    shared = _dot((gate * _sigmoid(gate) * up).astype(BF16), wd_ref[...])
    pltpu.make_async_copy(ybuf_ref, ybuf_ref, sem).wait()
    rw = rw_ref[...]
    routed = ybuf_ref[0] * rw[:, 0:1]
    for k in range(1, TOP_K):
        routed = routed + ybuf_ref[k] * rw[:, k:k + 1]
    out_ref[...] = _layer_norm(ALPHA * x_ref[...] + (shared + routed), g_ref[...], b_ref[...])


def _ffn(dest, x1, x1b, row_w, y, w_gate_s, w_up_s, w_down_s, ln_g, ln_b):
    n = x1.shape[0]
    tm = COMBINE_TILE
    row = pl.BlockSpec((tm, D_MODEL), lambda i: (i, 0))
    full = lambda *shape: pl.BlockSpec(shape, lambda i: (0,) * len(shape))
    return pl.pallas_call(
        _ffn_kernel,
        grid=(n // tm,),
        in_specs=[pl.BlockSpec((TOP_K, tm), lambda i: (0, i), memory_space=pltpu.SMEM),
                  row, row, pl.BlockSpec((tm, TOP_K), lambda i: (i, 0)), pl.BlockSpec(memory_space=pl.ANY),
                  full(D_MODEL, EXPERT_DIM), full(D_MODEL, EXPERT_DIM),
                  full(EXPERT_DIM, D_MODEL), full(1, D_MODEL), full(1, D_MODEL)],
        out_specs=row,
        out_shape=jax.ShapeDtypeStruct((n, D_MODEL), F32),
        scratch_shapes=[pltpu.VMEM((TOP_K, tm, D_MODEL), F32), pltpu.SemaphoreType.DMA(())],
        compiler_params=_cparams("arbitrary"),
        name="ffn",
    )(dest, x1, x1b, row_w, y, w_gate_s, w_up_s, w_down_s, ln_g, ln_b)


def _t5_bucket(dist):
    n = jnp.maximum(dist, 0)
    exact = T5_BUCKETS // 2
    nf = jnp.maximum(n, 1).astype(F32)
    large = exact + (jnp.log(nf / exact) / math.log(T5_MAX_DIST / exact) * (T5_BUCKETS - exact)).astype(jnp.int32)
    large = jnp.minimum(large, T5_BUCKETS - 1)
    return jnp.where(n < exact, n, large)


def _bias_by_distance(table, max_dist):
    return table[_t5_bucket(jnp.arange(max_dist + 1, dtype=jnp.int32))].T


def _toeplitz_tiles(bias_vec, dist, valid):
    safe = np.where(valid, dist, 0).astype(np.int32)
    tiles = bias_vec[:, safe]
    tiles = jnp.where(valid[None], tiles, NEG_INF)
    return jnp.moveaxis(tiles, 0, -3)


def _band_bias(table, max_dist, dil):
    blk = BAND_BLOCK
    r = np.arange(blk)[:, None]
    j = np.arange(2 * blk)[None, :]
    dist = np.stack([r - j, blk + r - j])
    valid = (dist >= 0) & (dist <= max_dist)
    vec = _bias_by_distance(table, max_dist * dil)[:, ::dil]
    return _toeplitz_tiles(vec, dist, valid)


def _diff_bias(table, seq):
    t, tk = FLASH_TILE, KV_TILE
    saturation = int(math.ceil((T5_BUCKETS // 2) * (T5_MAX_DIST / (T5_BUCKETS // 2)) ** ((T5_BUCKETS // 2 - 1) / (T5_BUCKETS // 2)) * 1.05))
    n_bias = min(seq // tk, -(-(saturation - 1) // tk) + 1 + KV_PER_Q)
    key = np.arange(tk)[:, None]
    query = np.arange(t)[None, :]
    dist = np.stack([(i - (KV_PER_Q - 1)) * tk + query - key for i in range(n_bias)])
    valid = dist >= 0
    vec = _bias_by_distance(table, min(seq, n_bias * tk + t))
    return _toeplitz_tiles(vec, dist, valid)


SWA_HEAD_ORDER = (0, 2, 1, 3)


def _pack_qkv_weights(w_in):
    offs = np.cumsum([0, 256, 256, 256, 4, 256, 128, 128, 256, 256, 256, 256, 256, 256])
    (qa, ka, va, fa, qb, kb, vb, qc, kc, vc, qd, kd, vd) = [int(o) for o in offs[:13]]
    gates = int(offs[13])
    col = lambda o, w: w_in[:, o:o + w]
    scale = HEAD_DIM ** -0.5
    qb_w = col(qb, 256).reshape(D_MODEL, N_HEADS, HEAD_DIM)[:, np.asarray(SWA_HEAD_ORDER)].reshape(D_MODEL, 256)
    w_rows = jnp.concatenate([
        col(ka, 256), qb_w * scale, col(kb, 128), col(vb, 128), col(kc, 256),
        col(qd, 256) * scale, col(kd, 256), col(vd, 256)], axis=1).astype(BF16)
    w_t = jnp.concatenate([col(qa, 256) * scale, col(va, 256), col(qc, 256), col(vc, 256)], axis=1).T.astype(BF16)
    w_f = jnp.pad(col(fa, 4), ((0, 0), (0, 124))).astype(BF16)
    w_gates = col(gates, 4 * D_MODEL).astype(BF16)
    return w_rows, w_t, w_f, w_gates


def _expert_layout(counts, n_tok):
    tb = EXPERT_TILE
    nblk = n_tok * TOP_K // tb + N_EXPERTS
    padded = (counts + tb - 1) // tb * tb
    pends = jnp.cumsum(padded)
    pstarts = (pends - padded).astype(jnp.int32)
    blk_e = jnp.minimum(jnp.searchsorted(pends, jnp.arange(nblk) * tb, side="right"), N_EXPERTS - 1).astype(jnp.int32)
    n_used = (pends[-1] // tb).astype(jnp.int32).reshape(1)
    return pstarts, blk_e, n_used, nblk * tb


def kernel(x, t5_table, w_in, b_forget, attn_sinks, lam_q1, lam_k1, lam_q2, lam_k2, diff_norm_g, w_branch, w_out, ln1_g, ln1_b, w_router, router_bias, w_gate_e, w_up_e, w_down_e, w_gate_s, w_up_s, w_down_s, ln2_g, ln2_b):
    bsz, seq, d = x.shape
    n_tok = bsz * seq
    t = FLASH_TILE
    table = t5_table.astype(F32)
    swa_bias = _band_bias(table[:, np.asarray(SWA_HEAD_ORDER)], SWA_WINDOW - 1, 1)
    diff_bias = _diff_bias(table[:, N_HEADS:2 * N_HEADS], seq) * LOG2_E
    dil_bias = [_band_bias(table[:, 2 * N_HEADS:], window // dil, dil) for window, dil in DIL_PATTERNS]

    x2d = x.reshape(n_tok, d)
    for layer in range(DEPTH):
        lam_init = 0.8 - 0.6 * math.exp(-0.3 * layer)
        w_rows, w_t, w_f, w_gates = _pack_qkv_weights(w_in[layer])
        h3, qt_a, vt_a, qt_c, vt_c, f_logit = _proj(x2d.reshape(bsz, seq, d), w_rows, w_t, w_f)

        log_f = jax.nn.log_sigmoid(f_logit[:, :, :N_HEADS] + b_forget[layer].astype(F32))
        c = jnp.cumsum(log_f, axis=1)
        cq = jnp.pad(c.transpose(0, 2, 1), ((0, 0), (0, 8 - N_HEADS), (0, 0)))
        ya = _fox(qt_a, h3, vt_a, cq, c).reshape(n_tok, BRANCH_WIDTH)

        sinks = attn_sinks[layer].astype(F32)[jnp.asarray(SWA_HEAD_ORDER)]
        (yb,) = _band(h3, swa_bias, sinks, dil=1, q_col=1, k_col=4, v_col=5, kvw=128, out_dtype=BF16)

        lam = (jnp.exp(jnp.sum(lam_q1[layer].astype(F32) * lam_k1[layer].astype(F32)))
               - jnp.exp(jnp.sum(lam_q2[layer].astype(F32) * lam_k2[layer].astype(F32))) + lam_init)
        g_col = (jnp.tile(diff_norm_g[layer].astype(F32), N_HEADS) * (1.0 - lam_init)).reshape(BRANCH_WIDTH, 1)
        yc = _diff(qt_c, h3, vt_c, lam.reshape(1), diff_bias, g_col).reshape(n_tok, BRANCH_WIDTH)

        od, ld = [], []
        for (window, dil), bias in zip(DIL_PATTERNS, dil_bias):
            o, lse = _band(h3, bias, None, dil=dil, q_col=4, k_col=5, v_col=6, kvw=256, out_dtype=F32)
            od.append(o)
            ld.append(lse)

        wb = w_branch[layer]
        wb = jnp.stack([wb[0], wb[1].reshape(N_HEADS, HEAD_DIM, d)[np.asarray(SWA_HEAD_ORDER)].reshape(BRANCH_WIDTH, d),
                        wb[2], wb[3]]).astype(BF16)
        x1, x1b = _merge(x2d, ya, yb, yc, od, ld, w_gates, wb, w_out[layer].astype(BF16),
                         ln1_g[layer].reshape(1, d), ln1_b[layer].reshape(1, d))

        idx_t, wts_t, rank_t, cnt = _router(x1b, w_router[layer].T.astype(BF16),
                                            router_bias[layer].astype(F32).reshape(N_EXPERTS, 1))
        pstarts, blk_e, n_used, rows = _expert_layout(cnt[:, 0].astype(jnp.int32), n_tok)
        dest = _dest(idx_t, rank_t, pstarts)
        xs = _dispatch(dest, x1, jnp.zeros((rows, d), F32))
        y = _experts(blk_e, n_used, xs, w_gate_e[layer].astype(BF16),
                     w_up_e[layer].astype(BF16), w_down_e[layer].astype(BF16))
        x2d = _ffn(dest, x1, x1b, wts_t.T, y, w_gate_s[layer].astype(BF16), w_up_s[layer].astype(BF16),
                   w_down_s[layer].astype(BF16), ln2_g[layer].reshape(1, d), ln2_b[layer].reshape(1, d))
    return x2d.reshape(bsz, seq, d)
```
